```python
import jax
import jax.numpy as jnp
from jax import lax
import numpy as np

D_MODEL = 2048
BATCH = 32
SEQ = 256
DEPTH = 2
DEC_BATCH = 8
DEC_SEQ = 1024
PAST_LEN = 512

GRID_W = 64
GLA_HEADS = 4
GLA_DK = 64
GLA_DV = 128
GLA_RANK = 16
GLA_TAU = 16.0
GLA_CHUNK = 32
FFT_GROUPS = 4
FFT_CH = 128
N_HEADS = 8
N_KV_HEADS = 2
HEAD_DIM = 128
GQA = N_HEADS // N_KV_HEADS
WINDOW = 128
ATT_BLOCK = 128
ROPE_BASE = 10000.0
D_FF = 5632
N_EXPERTS = 8
TOP_K = 2
MOE_D_FF = 5632
N_DENSE = (DEPTH + 1) // 2
N_MOE = DEPTH // 2
EPS = 1e-6
NEG_INF = -1e30

GLA_QK_W = GLA_HEADS * GLA_DK
GLA_V_W = GLA_HEADS * GLA_DV
FFT_W = FFT_GROUPS * FFT_CH
ATT_Q_W = N_HEADS * HEAD_DIM
ATT_KV_W = N_KV_HEADS * HEAD_DIM
IN_SPLITS = (GLA_QK_W, GLA_QK_W, GLA_V_W, GLA_V_W, 2 * GLA_RANK, FFT_W, ATT_Q_W, ATT_KV_W, ATT_KV_W)
IN_WIDTH = sum(IN_SPLITS)
MIX_WIDTH = GLA_V_W + FFT_W + ATT_Q_W

kernel_name = 'hybrid_gla_fnet_swa_diffusion_step'

F32 = jnp.float32


def _split_cols(z):
    idx = []
    acc = 0
    for w in IN_SPLITS[:-1]:
        acc += w
        idx.append(acc)
    return jnp.split(z, idx, axis=-1)


def rms_norm(x, w):
    xf = x.astype(F32)
    y = xf * lax.rsqrt(jnp.mean(xf * xf, axis=-1, keepdims=True) + EPS) * w.astype(F32)
    return y.astype(x.dtype)


def axial_rope(x):
    T = x.shape[1]
    rows = T // GRID_W
    r = jnp.repeat(jnp.arange(rows), GRID_W).astype(F32)
    col = jnp.tile(jnp.arange(GRID_W), rows).astype(F32)
    half = HEAD_DIM // 2
    inv = ROPE_BASE ** (-jnp.arange(0, half, 2, dtype=F32) / half)

    def rot(xh, pos):
        ang = pos[:, None] * inv[None, :]
        cos = jnp.cos(ang)[None, :, None, :]
        sin = jnp.sin(ang)[None, :, None, :]
        x1, x2 = jnp.split(xh.astype(F32), 2, axis=-1)
        return jnp.concatenate([x1 * cos - x2 * sin, x1 * sin + x2 * cos], axis=-1)

    return jnp.concatenate([rot(x[..., :half], r), rot(x[..., half:], col)], axis=-1).astype(x.dtype)


def gla_chunked(q, k, v, log_a, s0):
    B, T, H, DK = q.shape
    DV = v.shape[-1]
    C = GLA_CHUNK
    N = T // C

    def rs(a):
        return a.reshape(B, N, C, H, a.shape[-1]).astype(F32)

    q, k, v, g = rs(q), rs(k), rs(v), rs(log_a)
    b = jnp.cumsum(g, axis=2)
    b_last = b[:, :, -1]
    q_t = q * jnp.exp(b)
    k_t = k * jnp.exp(-b)
    k_end = k * jnp.exp(b_last[:, :, None] - b)
    causal = jnp.tril(jnp.ones((C, C), dtype=bool))
    A = jnp.einsum('bnihd,bnjhd->bnhij', q_t, k_t)
    A = jnp.where(causal, A, 0.0)
    o_intra = jnp.einsum('bnhij,bnjhe->bnihe', A, v)
    U = jnp.einsum('bnjhd,bnjhe->bnhde', k_end, v)
    decay = jnp.exp(b_last)

    def step(s, inp):
        u, d = inp
        return d[..., None] * s + u, s

    s_fin, s_starts = lax.scan(step, s0.astype(F32), (jnp.moveaxis(U, 1, 0), jnp.moveaxis(decay, 1, 0)))
    s_starts = jnp.moveaxis(s_starts, 0, 1)
    o_inter = jnp.einsum('bnihd,bnhde->bnihe', q_t, s_starts)
    o = (o_intra + o_inter).reshape(B, T, H, DV)
    return o, s_fin


def fourier_mix(h):
    B, T, _ = h.shape
    hf = h.astype(F32).reshape(B, T, FFT_GROUPS, FFT_CH)
    y = jnp.fft.fft2(hf, axes=(1, 3), norm='ortho').real
    return y.reshape(B, T, FFT_W).astype(h.dtype)


def context_attention(q, k, v, sink):
    B, S = q.shape[0], q.shape[1]
    nb = S // ATT_BLOCK
    scale = HEAD_DIM ** -0.5
    qb = jnp.moveaxis(q.reshape(B, nb, ATT_BLOCK, N_KV_HEADS, GQA, HEAD_DIM), 1, 0)
    sink_l = jnp.broadcast_to(sink.astype(F32).reshape(1, N_KV_HEADS, GQA, 1, 1), (B, N_KV_HEADS, GQA, ATT_BLOCK, 1))

    def one(qn):
        s = jnp.einsum('bqkgd,bskd->bkgqs', qn, k, preferred_element_type=F32) * scale
        p = jax.nn.softmax(jnp.concatenate([s, sink_l], axis=-1), axis=-1)[..., :S].astype(v.dtype)
        o = jnp.einsum('bkgqs,bskd->bqkgd', p, v)
        return o.reshape(B, ATT_BLOCK, N_HEADS, HEAD_DIM)

    o = lax.map(one, qb)
    return jnp.moveaxis(o, 0, 1).reshape(B, S, N_HEADS, HEAD_DIM)


def latent_attention(q, k, v, ctx_k, ctx_v, sink):
    B, T = q.shape[0], q.shape[1]
    P = ctx_k.shape[1]
    nb = T // ATT_BLOCK
    L = 3 * ATT_BLOCK
    scale = HEAD_DIM ** -0.5
    qb = jnp.moveaxis(q.reshape(B, nb, ATT_BLOCK, N_KV_HEADS, GQA, HEAD_DIM), 1, 0)

    def band(a):
        ap = jnp.pad(a, ((0, 0), (ATT_BLOCK, ATT_BLOCK), (0, 0), (0, 0)))
        ap = ap.reshape(B, nb + 2, ATT_BLOCK, N_KV_HEADS, HEAD_DIM)
        ab = jnp.concatenate([ap[:, :-2], ap[:, 1:-1], ap[:, 2:]], axis=2)
        return jnp.moveaxis(ab, 1, 0)

    kb, vb = band(k), band(v)
    q_off = jnp.arange(ATT_BLOCK)[:, None]
    k_off = jnp.arange(L)[None, :]
    in_window = jnp.abs(k_off - ATT_BLOCK - q_off) <= WINDOW
    sink_l = jnp.broadcast_to(sink.astype(F32).reshape(1, N_KV_HEADS, GQA, 1, 1), (B, N_KV_HEADS, GQA, ATT_BLOCK, 1))

    def one(args):
        n, qn, kn, vn = args
        kpos = (n - 1) * ATT_BLOCK + k_off
        valid = in_window & (kpos >= 0) & (kpos < T)
        s_loc = jnp.einsum('bqkgd,bskd->bkgqs', qn, kn, preferred_element_type=F32) * scale
        s_loc = jnp.where(valid, s_loc, NEG_INF)
        s_ctx = jnp.einsum('bqkgd,bskd->bkgqs', qn, ctx_k, preferred_element_type=F32) * scale
        p = jax.nn.softmax(jnp.concatenate([s_loc, s_ctx, sink_l], axis=-1), axis=-1).astype(v.dtype)
        o = (jnp.einsum('bkgqs,bskd->bqkgd', p[..., :L], vn)
             + jnp.einsum('bkgqs,bskd->bqkgd', p[..., L:L + P], ctx_v))
        return o.reshape(B, ATT_BLOCK, N_HEADS, HEAD_DIM)

    o = lax.map(one, (jnp.arange(nb), qb, kb, vb))
    return jnp.moveaxis(o, 0, 1).reshape(B, T, N_HEADS, HEAD_DIM)


def token_mixer(h, mixer_p, ctx_k, ctx_v, s0):
    w_in, w_lr2, b_lr2, gla_norm_w, q_norm_w, k_norm_w, sink, w_out = mixer_p
    B, T, _ = h.shape
    z = h @ w_in
    gq, gk, gv, gg, glr, zf, aq, ak, av = _split_cols(z)

    q = gq.reshape(B, T, GLA_HEADS, GLA_DK) * (GLA_DK ** -0.5)
    k = gk.reshape(B, T, GLA_HEADS, GLA_DK)
    v = gv.reshape(B, T, GLA_HEADS, GLA_DV)
    lr_f, lr_b = jnp.split(glr, 2, axis=-1)

    def log_decay(lr, w, b):
        return (jax.nn.log_sigmoid((lr @ w + b).astype(F32)) / GLA_TAU).reshape(B, T, GLA_HEADS, GLA_DK)

    la_f = log_decay(lr_f, w_lr2[0], b_lr2[0])
    la_b = log_decay(lr_b, w_lr2[1], b_lr2[1])
    if s0 is None:
        zero = jnp.zeros((B, GLA_HEADS, GLA_DK, GLA_DV), F32)
        s0 = (zero, zero)
    o_f, s_f = gla_chunked(q, k, v, la_f, s0[0])
    flip = lambda a: jnp.flip(a, axis=1)
    o_b, s_b = gla_chunked(flip(q), flip(k), flip(v), flip(la_b), s0[1])
    o = o_f + flip(o_b)
    o = rms_norm(o, gla_norm_w) * jax.nn.silu(gg.reshape(B, T, GLA_HEADS, GLA_DV).astype(F32))
    y_gla = o.reshape(B, T, GLA_V_W).astype(h.dtype)

    y_fft = fourier_mix(zf)

    aq = rms_norm(aq.reshape(B, T, N_HEADS, HEAD_DIM), q_norm_w)
    ak = rms_norm(ak.reshape(B, T, N_KV_HEADS, HEAD_DIM), k_norm_w)
    av = av.reshape(B, T, N_KV_HEADS, HEAD_DIM)
    if ctx_k is None:
        y_att = context_attention(aq, ak, av, sink)
    else:
        y_att = latent_attention(axial_rope(aq), axial_rope(ak), av, ctx_k, ctx_v, sink)

    y = jnp.concatenate([y_gla, y_fft, y_att.reshape(B, T, ATT_Q_W)], axis=-1) @ w_out
    state = jnp.stack([s_f, s_b], axis=1).astype(h.dtype)
    return y, ak, av, state


def swiglu(h, w1, w3, w2):
    return (jax.nn.silu(h @ w1) * (h @ w3)) @ w2


def moe_swiglu(h, router, w1, w3, w2):
    B, T, D = h.shape
    xf = h.reshape(B * T, D)
    logits = (xf @ router).astype(F32)
    top_v, top_i = lax.top_k(logits, TOP_K)
    gates = jax.nn.softmax(top_v, axis=-1)
    combine = jnp.sum(jax.nn.one_hot(top_i, N_EXPERTS, dtype=F32) * gates[..., None], axis=1).astype(h.dtype)
    out = jnp.zeros_like(xf)
    for e in range(N_EXPERTS):
        out = out + combine[:, e:e + 1] * swiglu(xf, w1[e], w3[e], w2[e])
    return out.reshape(B, T, D)


def trunk_layer(x, cond, ada_p, mixer_p, ffn_p, ctx_k, ctx_v, s0):
    w_ada, b_ada, n1, n2 = ada_p
    mod = jax.nn.silu(cond) @ w_ada + b_ada
    if mod.ndim == 2:
        mod = mod[:, None, :]
    shift1, scale1, gate1, shift2, scale2, gate2 = jnp.split(mod, 6, axis=-1)
    h = rms_norm(x, n1) * (1.0 + scale1) + shift1
    y, k, v, state = token_mixer(h, mixer_p, ctx_k, ctx_v, s0)
    x = x + gate1 * y
    h = rms_norm(x, n2) * (1.0 + scale2) + shift2
    router, w1, w3, w2 = ffn_p
    f = swiglu(h, w1, w3, w2) if router is None else moe_swiglu(h, router, w1, w3, w2)
    x = x + gate2 * f
    return x, k, v, state


def setup_inputs(seed: int = 0) -> dict:
    key = jax.random.key(seed)
    ks = jax.random.split(key, 32)
    D = D_MODEL

    def nrm(k, shape, s):
        return jax.random.normal(k, shape, F32) * s

    return {
        'x_prompt': nrm(ks[0], (BATCH, SEQ, D), 1.0),
        'x_sample': nrm(ks[1], (DEC_BATCH, DEC_SEQ, D), 1.0),
        'cache_k': nrm(ks[2], (DEC_BATCH, DEPTH, PAST_LEN, N_KV_HEADS, HEAD_DIM), 1.0),
        'cache_v': nrm(ks[3], (DEC_BATCH, DEPTH, PAST_LEN, N_KV_HEADS, HEAD_DIM), 1.0),
        'state_gla': nrm(ks[4], (DEC_BATCH, DEPTH, 2, GLA_HEADS, GLA_DK, GLA_DV), 0.5),
        'c': nrm(ks[5], (DEC_BATCH, D), 1.0),
        'c_ctx': nrm(ks[6], (D,), 1.0),
        'w_ada': nrm(ks[7], (DEPTH, D, 6 * D), D ** -0.5),
        'b_ada': nrm(ks[8], (DEPTH, 6 * D), 0.02),
        'norm1_w': 1.0 + nrm(ks[9], (DEPTH, D), 0.02),
        'norm2_w': 1.0 + nrm(ks[10], (DEPTH, D), 0.02),
        'w_in': nrm(ks[11], (DEPTH, D, IN_WIDTH), D ** -0.5),
        'w_gla_lr2': nrm(ks[12], (DEPTH, 2, GLA_RANK, GLA_QK_W), GLA_RANK ** -0.5),
        'b_gla_lr2': nrm(ks[13], (DEPTH, 2, GLA_QK_W), 0.1),
        'gla_norm_w': 1.0 + nrm(ks[14], (DEPTH, GLA_DV), 0.02),
        'q_norm_w': 1.0 + nrm(ks[15], (DEPTH, HEAD_DIM), 0.02),
        'k_norm_w': 1.0 + nrm(ks[16], (DEPTH, HEAD_DIM), 0.02),
        'attn_sink': nrm(ks[17], (DEPTH, N_HEADS), 0.5),
        'w_out': nrm(ks[18], (DEPTH, MIX_WIDTH, D), MIX_WIDTH ** -0.5),
        'ffn_w1': nrm(ks[19], (N_DENSE, D, D_FF), D ** -0.5),
        'ffn_w3': nrm(ks[20], (N_DENSE, D, D_FF), D ** -0.5),
        'ffn_w2': nrm(ks[21], (N_DENSE, D_FF, D), D_FF ** -0.5),
        'moe_router': nrm(ks[22], (N_MOE, D, N_EXPERTS), D ** -0.5),
        'moe_w1': nrm(ks[23], (N_MOE, N_EXPERTS, D, MOE_D_FF), D ** -0.5),
        'moe_w3': nrm(ks[24], (N_MOE, N_EXPERTS, D, MOE_D_FF), D ** -0.5),
        'moe_w2': nrm(ks[25], (N_MOE, N_EXPERTS, MOE_D_FF, D), MOE_D_FF ** -0.5),
    }


def reference(x_prompt, x_sample, cache_k, cache_v, state_gla, c, c_ctx, w_ada, b_ada, norm1_w, norm2_w,
              w_in, w_gla_lr2, b_gla_lr2, gla_norm_w, q_norm_w, k_norm_w, attn_sink, w_out,
              ffn_w1, ffn_w3, ffn_w2, moe_router, moe_w1, moe_w3, moe_w2):
    xp = x_prompt
    xs = x_sample
    ks, vs, ss = [], [], []
    for l in range(DEPTH):
        ada_p = (w_ada[l], b_ada[l], norm1_w[l], norm2_w[l])
        mixer_p = (w_in[l], w_gla_lr2[l], b_gla_lr2[l], gla_norm_w[l], q_norm_w[l], k_norm_w[l], attn_sink[l], w_out[l])
        if l % 2 == 0:
            ffn_p = (None, ffn_w1[l // 2], ffn_w3[l // 2], ffn_w2[l // 2])
        else:
            ffn_p = (moe_router[l // 2], moe_w1[l // 2], moe_w3[l // 2], moe_w2[l // 2])
        xp, k_l, v_l, s_l = trunk_layer(xp, c_ctx, ada_p, mixer_p, ffn_p, None, None, None)
        ks.append(k_l)
        vs.append(v_l)
        ss.append(s_l)
        xs, _, _, _ = trunk_layer(xs, c, ada_p, mixer_p, ffn_p, cache_k[:, l], cache_v[:, l],
                                  (state_gla[:, l, 0], state_gla[:, l, 1]))
    new_cache_k = jnp.stack(ks, axis=1)
    new_cache_v = jnp.stack(vs, axis=1)
    new_state_gla = jnp.stack(ss, axis=1)
    return (xp, xs, new_cache_k, new_cache_v, new_state_gla)
```

```python
import functools

import numpy as np
import jax
import jax.numpy as jnp
from jax import lax
from jax.experimental import pallas as pl
from jax.experimental.pallas import tpu as pltpu

F32 = jnp.float32
BF16 = jnp.bfloat16

D = 2048
N_CTX_B, CTX_T = 32, 256
N_LAT_B, LAT_T = 8, 1024
N_CTX = N_CTX_B * CTX_T
N_LAT = N_LAT_B * LAT_T
N_TOK = N_CTX + N_LAT
DEPTH = 2
PAST = 512
GRID_W = 64
GLA_H, GLA_DK, GLA_DV, GLA_RANK, GLA_TAU, GLA_C = 4, 64, 128, 16, 16.0, 32
FFT_G, FFT_CH = 4, 128
N_HEADS, N_KV, HEAD_DIM = 8, 2, 128
GQA = N_HEADS // N_KV
WINDOW, ATT_BLOCK = 128, 128
ROPE_BASE = 10000.0
D_FF = 5632
N_EXP = 8
EPS = 1e-6
NEG_INF = -1e30

Z_GQ, Z_GK, Z_GV, Z_GG, Z_FF, Z_AQ, Z_AK, Z_AV, Z_LR = 0, 256, 512, 1024, 1536, 2048, 3072, 3328, 3584
Z_W = 3840
LANE = 128

VMEM_LIMIT = 56 * 1024 * 1024

MOE_TM = 1024
MOE_TILES = 2 * N_TOK // MOE_TM + N_EXP
MOE_ROWS = MOE_TILES * MOE_TM


def _cparams(sem):
    return pltpu.CompilerParams(dimension_semantics=sem, vmem_limit_bytes=VMEM_LIMIT)


def _silu(x):
    return x / (1.0 + jnp.exp(-x))


def _norm_mod(x, nw, scale, shift):
    ms = jnp.mean(x * x, axis=-1, keepdims=True)
    return (x * lax.rsqrt(ms + EPS) * nw) * (1.0 + scale) + shift


def _dot(a, b):
    return jnp.dot(a, b, preferred_element_type=F32)


def _dot_nt(a, b):
    return lax.dot_general(a, b, (((1,), (1,)), ((), ())), preferred_element_type=F32)


def _dot_tn(a, b):
    return lax.dot_general(a, b, (((0,), (0,)), ((), ())), preferred_element_type=F32)


def _mod_body(c_ref, w_ref, b_ref, o_ref):
    s = _silu(c_ref[...])
    o_ref[...] = _dot(s.astype(BF16), w_ref[...].astype(BF16)) + b_ref[...]


def _mods(cond16, w_ada, b_ada):
    tn = 1024
    return pl.pallas_call(
        _mod_body,
        grid=(DEPTH, 6 * D // tn),
        in_specs=[pl.BlockSpec((16, D), lambda l, j: (0, 0)),
                  pl.BlockSpec((None, D, tn), lambda l, j: (l, 0, j)),
                  pl.BlockSpec((None, 1, tn), lambda l, j: (l, 0, j))],
        out_specs=pl.BlockSpec((None, 16, tn), lambda l, j: (l, 0, j)),
        out_shape=jax.ShapeDtypeStruct((DEPTH, 16, 6 * D), F32),
        compiler_params=_cparams(("arbitrary", "arbitrary")),
        name="adaln_mod",
    )(cond16, w_ada, b_ada.reshape(DEPTH, 1, 6 * D))


def _mod_row(i, tm):
    per_lat = LAT_T // tm
    n_ctx_tiles = N_CTX // tm
    return jnp.where(i < n_ctx_tiles, 0, 1 + (i - n_ctx_tiles) // per_lat)


def _split_x(x, tm):
    if isinstance(x, tuple):
        return x[0], x[1], 0
    return x, x, N_CTX // tm


def _x_specs(tm, off_b, width):
    n_ctx_tiles = N_CTX // tm
    return [pl.BlockSpec((tm, width), lambda i: (jnp.minimum(i, n_ctx_tiles - 1), 0)),
            pl.BlockSpec((tm, width), lambda i: (jnp.maximum(i - n_ctx_tiles, 0) + off_b, 0))]


def _pick_x(xa_ref, xb_ref, tm):
    return jnp.where(pl.program_id(0) < N_CTX // tm, xa_ref[...], xb_ref[...])


def _inproj_body(xa_ref, xb_ref, mod_ref, n_ref, w_ref, o_ref, *, tm):
    h = _norm_mod(_pick_x(xa_ref, xb_ref, tm), n_ref[...], mod_ref[1:2, :], mod_ref[0:1, :])
    o_ref[...] = _dot(h.astype(BF16), w_ref[...]).astype(BF16)


def _inproj(x, mod, n1, w_in_b):
    tm = 512
    xa, xb, off_b = _split_x(x, tm)
    return pl.pallas_call(
        functools.partial(_inproj_body, tm=tm),
        grid=(N_TOK // tm,),
        in_specs=_x_specs(tm, off_b, D) + [
            pl.BlockSpec((None, 6, D), lambda i: (_mod_row(i, tm), 0, 0)),
            pl.BlockSpec((1, D), lambda i: (0, 0)),
            pl.BlockSpec((D, Z_W), lambda i: (0, 0), pipeline_mode=pl.Buffered(1))],
        out_specs=pl.BlockSpec((tm, Z_W), lambda i: (i, 0)),
        out_shape=jax.ShapeDtypeStruct((N_TOK, Z_W), BF16),
        compiler_params=_cparams(("arbitrary",)),
        name="inproj",
    )(xa, xb, mod, n1.reshape(1, D), w_in_b)


def _gla_body(qk_ref, v_ref, gg_ref, lr_ref, wlr_ref, blr_ref, s0_ref, nw_ref, y_ref, st_ref,
              g_s, o_s, st_s, qt_s, ke_s, dec_s, *, T):
    C = GLA_C
    NC = T // C
    qw = GLA_H * GLA_DK
    lr = lr_ref[...]
    for d in range(2):
        x = _dot(lr, wlr_ref[d]) + blr_ref[d]
        g_s[d] = (jnp.minimum(x, 0.0) - jnp.log1p(jnp.exp(-jnp.abs(x)))) * (1.0 / GLA_TAU)
    st_s[...] = s0_ref[...]

    R = 128
    cpb = R // C
    ri = lax.broadcasted_iota(jnp.int32, (R, R), 0)
    ci = lax.broadcasted_iota(jnp.int32, (R, R), 1)
    same = (ri // C) == (ci // C)
    masks = (jnp.logical_and(same, ci <= ri), jnp.logical_and(same, ci >= ri))
    tris = tuple(m.astype(F32) for m in masks)
    ones_blk = same.astype(F32)

    def intra(i, carry):
        r0 = pl.multiple_of(i * R, R)
        q = qk_ref[pl.ds(r0, R), 0:qw].astype(F32) * (GLA_DK ** -0.5)
        k = qk_ref[pl.ds(r0, R), qw:2 * qw].astype(F32)
        vb = v_ref[pl.ds(r0, R), :]
        for d in range(2):
            g = g_s[d, pl.ds(r0, R), :]
            b = jnp.dot(tris[d], g, precision=lax.Precision.HIGHEST, preferred_element_type=F32)
            bl = jnp.dot(ones_blk, g, precision=lax.Precision.HIGHEST, preferred_element_type=F32)
            qt = (q * jnp.exp(b)).astype(BF16)
            kt = (k * jnp.exp(-b)).astype(BF16)
            qt_s[d, pl.ds(r0, R), :] = qt
            ke_s[d, pl.ds(r0, R), :] = (k * jnp.exp(bl - b)).astype(BF16)
            ebl = jnp.exp(bl)
            for j in range(cpb):
                dec_s[d, pl.ds(i * cpb + j, 1), :] = ebl[j * C:j * C + 1, :]
            outs = []
            for h in range(GLA_H):
                ks = slice(h * GLA_DK, (h + 1) * GLA_DK)
                a = jnp.where(masks[d], _dot_nt(qt[:, ks], kt[:, ks]), 0.0)
                outs.append(_dot(a.astype(BF16), vb[:, h * GLA_DV:(h + 1) * GLA_DV]))
            o_s[d, pl.ds(r0, R), :] = jnp.concatenate(outs, axis=1)
        return carry

    lax.fori_loop(0, T // R, intra, 0)

    def inter(i, carry):
        for d in range(2):
            c = i if d == 0 else NC - 1 - i
            r0 = pl.multiple_of(c * C, C)
            qt = qt_s[d, pl.ds(r0, C), :]
            ke = ke_s[d, pl.ds(r0, C), :]
            vb = v_ref[pl.ds(r0, C), :]
            dec = dec_s[d, pl.ds(c, 1), :]
            outs = []
            for h in range(GLA_H):
                ks = slice(h * GLA_DK, (h + 1) * GLA_DK)
                s_t = st_s[d, h]
                outs.append(_dot_nt(qt[:, ks], s_t.astype(BF16)))
                u_t = _dot_tn(vb[:, h * GLA_DV:(h + 1) * GLA_DV], ke[:, ks])
                st_s[d, h] = s_t * dec[:, ks] + u_t
            o_s[d, pl.ds(r0, C), :] += jnp.concatenate(outs, axis=1)
        return carry

    lax.fori_loop(0, NC, inter, 0)

    ch = 128
    nw = nw_ref[...]

    def epi(i, carry):
        r = pl.multiple_of(i * ch, ch)
        for h in range(GLA_H):
            cs = slice(h * GLA_DV, (h + 1) * GLA_DV)
            o = o_s[0, pl.ds(r, ch), cs] + o_s[1, pl.ds(r, ch), cs]
            ms = jnp.mean(o * o, axis=-1, keepdims=True)
            y = (o * lax.rsqrt(ms + EPS) * nw) * _silu(gg_ref[pl.ds(r, ch), cs].astype(F32))
            y_ref[pl.ds(r, ch), cs] = y.astype(BF16)
        return carry

    lax.fori_loop(0, T // ch, epi, 0)
    st_ref[...] = st_s[...]


def _gla(z, wlr, blr, s0_t, nw, *, T, n_units, unit0):
    rb = lambda u: u + unit0
    hv = GLA_H * GLA_DV
    qw = GLA_H * GLA_DK
    in_specs = [pl.BlockSpec((T, 2 * qw), lambda u: (rb(u), Z_GQ // (2 * qw))),
                pl.BlockSpec((T, hv), lambda u: (rb(u), Z_GV // hv)),
                pl.BlockSpec((T, hv), lambda u: (rb(u), Z_GG // hv)),
                pl.BlockSpec((T, LANE), lambda u: (rb(u), Z_LR // LANE)),
                pl.BlockSpec((2, LANE, qw), lambda u: (0, 0, 0)),
                pl.BlockSpec((2, 1, qw), lambda u: (0, 0, 0)),
                pl.BlockSpec((None, 2, GLA_H, GLA_DV, GLA_DK), lambda u: (u, 0, 0, 0, 0)),
                pl.BlockSpec((1, GLA_DV), lambda u: (0, 0))]
    args = [z, z, z, z, wlr, blr, s0_t, nw.reshape(1, GLA_DV)]
    return pl.pallas_call(
        functools.partial(_gla_body, T=T),
        grid=(n_units,),
        in_specs=in_specs,
        out_specs=[pl.BlockSpec((T, hv), lambda u: (u, 0)),
                   pl.BlockSpec((None, 2, GLA_H, GLA_DV, GLA_DK), lambda u: (u, 0, 0, 0, 0))],
        out_shape=[jax.ShapeDtypeStruct((n_units * T, hv), BF16),
                   jax.ShapeDtypeStruct((n_units, 2, GLA_H, GLA_DV, GLA_DK), F32)],
        scratch_shapes=[pltpu.VMEM((2, T, qw), F32),
                        pltpu.VMEM((2, T, hv), F32),
                        pltpu.VMEM((2, GLA_H, GLA_DV, GLA_DK), F32),
                        pltpu.VMEM((2, T, qw), BF16),
                        pltpu.VMEM((2, T, qw), BF16),
                        pltpu.VMEM((2, T // GLA_C, qw), F32)],
        compiler_params=_cparams(("arbitrary",)),
        name=f"gla_T{T}",
    )(*args)


def _fft_body(x_ref, w2_ref, dt_ref, y_ref, p_s, *, T):
    xb = x_ref[...]
    for g in range(FFT_G):
        cs = slice(g * FFT_CH, (g + 1) * FFT_CH)
        p = _dot(xb[:, cs], w2_ref[...])
        p_s[0:T, cs] = p[:, :FFT_CH].astype(BF16)
        p_s[T:2 * T, cs] = p[:, FFT_CH:].astype(BF16)
    y_ref[...] = _dot(dt_ref[...], p_s[...]).astype(BF16)


def _dft_consts(T):
    c = np.arange(FFT_CH)
    ang_c = (np.outer(c, c) % FFT_CH) * (2.0 * np.pi / FFT_CH)
    w2 = np.concatenate([np.cos(ang_c), np.sin(ang_c)], axis=1) / np.sqrt(FFT_CH)
    t = np.arange(T)
    ang_t = (np.outer(t, t) % T) * (2.0 * np.pi / T)
    dt = np.concatenate([np.cos(ang_t), -np.sin(ang_t)], axis=1) / np.sqrt(T)
    return jnp.asarray(w2, F32).astype(BF16), jnp.asarray(dt, F32).astype(BF16)


def _fft(z, *, T, n_units, unit0):
    w2, dt = _dft_consts(T)
    fw = FFT_G * FFT_CH
    in_specs = [pl.BlockSpec((T, fw), lambda u: (u + unit0, Z_FF // fw)),
                pl.BlockSpec((FFT_CH, 2 * FFT_CH), lambda u: (0, 0)),
                pl.BlockSpec((T, 2 * T), lambda u: (0, 0))]
    args = [z, w2, dt]
    return pl.pallas_call(
        functools.partial(_fft_body, T=T),
        grid=(n_units,),
        in_specs=in_specs,
        out_specs=pl.BlockSpec((T, fw), lambda u: (u, 0)),
        out_shape=jax.ShapeDtypeStruct((n_units * T, fw), BF16),
        scratch_shapes=[pltpu.VMEM((2 * T, fw), BF16)],
        compiler_params=_cparams(("arbitrary",)),
        name=f"fft_T{T}",
    )(*args)


def _rope_tables(T):
    half = HEAD_DIM // 2
    inv = ROPE_BASE ** (-np.arange(0, half, 2, dtype=np.float64) / half)
    t = np.arange(T)
    ang_r = (t // GRID_W)[:, None] * inv[None, :]
    ang_c = (t % GRID_W)[:, None] * inv[None, :]
    cos = np.concatenate([np.cos(ang_r)] * 2 + [np.cos(ang_c)] * 2, axis=1)
    sin = np.concatenate([-np.sin(ang_r), np.sin(ang_r), -np.sin(ang_c), np.sin(ang_c)], axis=1)
    return jnp.asarray(cos, F32), jnp.asarray(sin, F32)


def _rope(x, cos, sin):
    lane = lax.broadcasted_iota(jnp.int32, x.shape, 1)
    quarter = HEAD_DIM // 4
    partner = jnp.where(lane % (2 * quarter) < quarter,
                        pltpu.roll(x, HEAD_DIM - quarter, 1), pltpu.roll(x, quarter, 1))
    return x * cos + partner * sin


def _rms128(x, w):
    return x * lax.rsqrt(jnp.mean(x * x, axis=-1, keepdims=True) + EPS) * w


def _attn_body(*refs, T, latent):
    if latent:
        (sink_ref, q_ref, k_ref, v_ref, qw_ref, kw_ref, ck_ref, cv_ref, cos_ref, sin_ref, y_ref) = refs
    else:
        (sink_ref, q_ref, k_ref, v_ref, qw_ref, kw_ref, y_ref, ko_ref, vo_ref) = refs
    kv = pl.program_id(1)
    nb = T // ATT_BLOCK
    kn = _rms128(k_ref[...].astype(F32), kw_ref[...])
    vb = v_ref[...]
    if latent:
        kn = _rope(kn, cos_ref[...], sin_ref[...])
        ckb = ck_ref[...].astype(BF16)
        cvb = cv_ref[...].astype(BF16)
    else:
        ko_ref[...] = kn
        vo_ref[...] = vb.astype(F32)
    kb = kn.astype(BF16)
    rows_g = lax.broadcasted_iota(jnp.int32, (GQA * ATT_BLOCK, 1), 0) // ATT_BLOCK
    sink = jnp.zeros((GQA * ATT_BLOCK, 1), F32)
    for g in range(GQA):
        sink = jnp.where(rows_g == g, sink_ref[kv * GQA + g], sink)
    scale = HEAD_DIM ** -0.5
    for n in range(nb):
        rs = slice(n * ATT_BLOCK, (n + 1) * ATT_BLOCK)
        qs = []
        for g in range(GQA):
            qn = _rms128(q_ref[rs, g * HEAD_DIM:(g + 1) * HEAD_DIM].astype(F32), qw_ref[...])
            if latent:
                qn = _rope(qn, cos_ref[rs, :], sin_ref[rs, :])
            qs.append((qn * scale).astype(BF16))
        qq = jnp.concatenate(qs, axis=0)
        if latent:
            lo, hi = max(n - 1, 0), min(n + 2, nb)
            ks = slice(lo * ATT_BLOCK, hi * ATT_BLOCK)
            w = (hi - lo) * ATT_BLOCK
            s_loc = _dot_nt(qq, kb[ks])
            qpos = n * ATT_BLOCK + lax.broadcasted_iota(jnp.int32, (GQA * ATT_BLOCK, w), 0) % ATT_BLOCK
            kpos = lo * ATT_BLOCK + lax.broadcasted_iota(jnp.int32, (GQA * ATT_BLOCK, w), 1)
            s_loc = jnp.where(jnp.abs(kpos - qpos) <= WINDOW, s_loc, NEG_INF)
            s_ctx = _dot_nt(qq, ckb)
            m = jnp.maximum(jnp.maximum(jnp.max(s_loc, axis=-1, keepdims=True),
                                        jnp.max(s_ctx, axis=-1, keepdims=True)), sink)
            p_loc = jnp.exp(s_loc - m)
            p_ctx = jnp.exp(s_ctx - m)
            den = (jnp.sum(p_loc, axis=-1, keepdims=True) + jnp.sum(p_ctx, axis=-1, keepdims=True)
                   + jnp.exp(sink - m))
            o = _dot(p_loc.astype(BF16), vb[ks]) + _dot(p_ctx.astype(BF16), cvb)
        else:
            s = _dot_nt(qq, kb)
            m = jnp.maximum(jnp.max(s, axis=-1, keepdims=True), sink)
            p = jnp.exp(s - m)
            den = jnp.sum(p, axis=-1, keepdims=True) + jnp.exp(sink - m)
            o = _dot(p.astype(BF16), vb)
        o = o * (1.0 / den)
        for g in range(GQA):
            y_ref[rs, g * HEAD_DIM:(g + 1) * HEAD_DIM] = o[g * ATT_BLOCK:(g + 1) * ATT_BLOCK].astype(BF16)


def _attn(z, sink_l, qw, kw, *, T, n_units, unit0, ctx_k=None, ctx_v=None):
    latent = ctx_k is not None
    qwid = GQA * HEAD_DIM
    rb = lambda u: u + unit0
    smem = pl.BlockSpec(memory_space=pltpu.SMEM)
    in_specs = [smem,
                pl.BlockSpec((T, qwid), lambda u, h: (rb(u), Z_AQ // qwid + h)),
                pl.BlockSpec((T, HEAD_DIM), lambda u, h: (rb(u), Z_AK // HEAD_DIM + h)),
                pl.BlockSpec((T, HEAD_DIM), lambda u, h: (rb(u), Z_AV // HEAD_DIM + h)),
                pl.BlockSpec((1, HEAD_DIM), lambda u, h: (0, 0)),
                pl.BlockSpec((1, HEAD_DIM), lambda u, h: (0, 0))]
    args = [sink_l, z, z, z, qw.reshape(1, HEAD_DIM), kw.reshape(1, HEAD_DIM)]
    y_spec = pl.BlockSpec((T, qwid), lambda u, h: (u, h))
    y_shape = jax.ShapeDtypeStruct((n_units * T, N_HEADS * HEAD_DIM), BF16)
    if latent:
        cos, sin = _rope_tables(T)
        in_specs += [pl.BlockSpec((None, PAST, HEAD_DIM), lambda u, h: (u, 0, h)),
                     pl.BlockSpec((None, PAST, HEAD_DIM), lambda u, h: (u, 0, h)),
                     pl.BlockSpec((T, HEAD_DIM), lambda u, h: (0, 0)),
                     pl.BlockSpec((T, HEAD_DIM), lambda u, h: (0, 0))]
        args += [ctx_k, ctx_v, cos, sin]
        out_specs, out_shape = y_spec, y_shape
    else:
        kv_spec = pl.BlockSpec((T, HEAD_DIM), lambda u, h: (u, h))
        kv_shape = jax.ShapeDtypeStruct((n_units * T, N_KV * HEAD_DIM), F32)
        out_specs, out_shape = [y_spec, kv_spec, kv_spec], [y_shape, kv_shape, kv_shape]
    return pl.pallas_call(
        functools.partial(_attn_body, T=T, latent=latent),
        grid=(n_units, N_KV),
        in_specs=in_specs,
        out_specs=out_specs,
        out_shape=out_shape,
        compiler_params=_cparams(("arbitrary", "arbitrary")),
        name=f"attn_T{T}",
    )(*args)


def _outproj_body(*refs, router, tm):
    y_refs, refs = refs[:6], refs[6:]
    if router:
        (w_ref, xa_ref, xb_ref, mod_ref, n2_ref, r_ref, x1_ref, h2_ref, idx_ref, gate_ref) = refs
    else:
        (w_ref, xa_ref, xb_ref, mod_ref, n2_ref, x1_ref, h2_ref) = refs
    gw = GLA_H * GLA_DV
    fw = FFT_G * FFT_CH
    yg, yf, ya = (_pick_x(y_refs[2 * j], y_refs[2 * j + 1], tm) for j in range(3))
    acc = _dot(yg, w_ref[0:gw, :]) + _dot(yf, w_ref[gw:gw + fw, :]) + _dot(ya, w_ref[gw + fw:, :])
    x1 = _pick_x(xa_ref, xb_ref, tm) + mod_ref[2:3, :] * acc
    x1_ref[...] = x1
    h2 = _norm_mod(x1, n2_ref[...], mod_ref[4:5, :], mod_ref[3:4, :])
    h2_ref[...] = h2.astype(h2_ref.dtype)
    if router:
        logits = _dot(h2.astype(BF16), r_ref[...])
        lane = lax.broadcasted_iota(jnp.int32, logits.shape, 1)
        lg = jnp.where(lane < N_EXP, logits, -jnp.inf)
        v1 = jnp.max(lg, axis=-1, keepdims=True)
        i1 = jnp.min(jnp.where(lg == v1, lane, LANE), axis=-1, keepdims=True)
        lg2 = jnp.where(lane == i1, -jnp.inf, lg)
        v2 = jnp.max(lg2, axis=-1, keepdims=True)
        i2 = jnp.min(jnp.where(lg2 == v2, lane, LANE), axis=-1, keepdims=True)
        e = jnp.exp(v2 - v1)
        g1 = 1.0 / (1.0 + e)
        idx_ref[...] = jnp.where(lane == 0, i1, jnp.where(lane == 1, i2, 0))
        gate_ref[...] = jnp.where(lane == 0, g1, jnp.where(lane == 1, e * g1, 0.0))


def _outproj(y_gla, y_fft, y_att, w_out_b, x, mod, n2, router_b=None):
    tm = 512
    router = router_b is not None
    gw, fw, aw = GLA_H * GLA_DV, FFT_G * FFT_CH, N_HEADS * HEAD_DIM
    row = lambda i: (i, 0)
    xa, xb, off_b = _split_x(x, tm)
    in_specs, args = [], []
    for pair, wid in ((y_gla, gw), (y_fft, fw), (y_att, aw)):
        in_specs += _x_specs(tm, 0, wid)
        args += list(pair)
    in_specs.append(pl.BlockSpec((gw + fw + aw, D), lambda i: (0, 0), pipeline_mode=pl.Buffered(1)))
    in_specs += _x_specs(tm, off_b, D)
    in_specs += [pl.BlockSpec((None, 6, D), lambda i: (_mod_row(i, tm), 0, 0)),
                 pl.BlockSpec((1, D), lambda i: (0, 0))]
    args += [w_out_b, xa, xb, mod, n2.reshape(1, D)]
    out_specs = [pl.BlockSpec((tm, D), row), pl.BlockSpec((tm, D), row)]
    out_shape = [jax.ShapeDtypeStruct((N_TOK, D), F32),
                 jax.ShapeDtypeStruct((N_TOK, D), F32 if router else BF16)]
    if router:
        in_specs.append(pl.BlockSpec((D, LANE), lambda i: (0, 0)))
        args.append(router_b)
        out_specs += [pl.BlockSpec((tm, LANE), row), pl.BlockSpec((tm, LANE), row)]
        out_shape += [jax.ShapeDtypeStruct((N_TOK, LANE), jnp.int32), jax.ShapeDtypeStruct((N_TOK, LANE), F32)]
    return pl.pallas_call(
        functools.partial(_outproj_body, router=router, tm=tm),
        grid=(N_TOK // tm,),
        in_specs=in_specs,
        out_specs=out_specs,
        out_shape=out_shape,
        compiler_params=_cparams(("arbitrary",)),
        name="outproj_router" if router else "outproj",
    )(*args)


FF_TF = 512
FF_KA = D_FF // FF_TF


def _store_f_slice(g_s, g, s):
    for kk in range(FF_KA):
        @pl.when(s == kk)
        def _(kk=kk):
            g_s[:, kk * FF_TF:(kk + 1) * FF_TF] = g


def _ffn_body(h_ref, w1_ref, w3_ref, w2_ref, x_ref, mod_ref, o_ref, g_s):
    s = pl.program_id(1)

    @pl.when(s < FF_KA)
    def _():
        h = h_ref[...]
        g = (_silu(_dot(h, w1_ref[...])) * _dot(h, w3_ref[...])).astype(BF16)
        _store_f_slice(g_s, g, s)

    @pl.when(s >= FF_KA)
    def _():
        o_ref[...] = x_ref[...] + mod_ref[5:6, :] * _dot(g_s[...], w2_ref[...])


def _ffn(h2, w1_b, w3_b, w2_b, x1, mod):
    tm, tn = 1024, 512
    kb = D // tn
    up = lambda i, s: (0, jnp.minimum(s, FF_KA - 1))
    down = lambda s: jnp.maximum(s - FF_KA, 0)
    return pl.pallas_call(
        _ffn_body,
        grid=(N_TOK // tm, FF_KA + kb),
        in_specs=[pl.BlockSpec((tm, D), lambda i, s: (i, 0)),
                  pl.BlockSpec((D, FF_TF), up),
                  pl.BlockSpec((D, FF_TF), up),
                  pl.BlockSpec((D_FF, tn), lambda i, s: (0, down(s))),
                  pl.BlockSpec((tm, tn), lambda i, s: (i, down(s))),
                  pl.BlockSpec((None, 6, tn), lambda i, s: (_mod_row(i, tm), 0, down(s)))],
        out_specs=pl.BlockSpec((tm, tn), lambda i, s: (i, down(s))),
        out_shape=jax.ShapeDtypeStruct((N_TOK, D), F32),
        scratch_shapes=[pltpu.VMEM((tm, D_FF), BF16)],
        compiler_params=_cparams(("arbitrary", "arbitrary")),
        name="ffn_dense",
    )(h2, w1_b, w3_b, w2_b, x1, mod)


MOE_TN = 256
MOE_KB = D // MOE_TN
MOE_GATHER = 512


def _moe_body(te_ref, nt_ref, src_ref, h_hbm, w1_ref, w3_ref, w2_ref, y_ref, xg_s, xb_s, g_s, sem):
    i = pl.program_id(0)
    s = pl.program_id(1)
    active = i < nt_ref[0]

    def row_copy(r, t):
        return pltpu.make_async_copy(h_hbm.at[pl.ds(t, 1)], xg_s.at[pl.ds(r, 1)], sem)

    @pl.when(jnp.logical_and(active, s == 0))
    def _():
        for part in range(MOE_TM // MOE_GATHER):
            base = i * MOE_TM + part * MOE_GATHER

            def issue(r, c):
                row_copy(r, src_ref[base + r]).start()
                return c

            lax.fori_loop(0, MOE_GATHER, issue, 0, unroll=8)
            pltpu.make_async_copy(h_hbm.at[pl.ds(0, MOE_GATHER)], xg_s, sem).wait()
            xb_s[part * MOE_GATHER:(part + 1) * MOE_GATHER, :] = xg_s[...].astype(BF16)

    @pl.when(jnp.logical_and(active, s < FF_KA))
    def _():
        h = xb_s[...]
        g = (_silu(_dot(h, w1_ref[...].astype(BF16))) * _dot(h, w3_ref[...].astype(BF16))).astype(BF16)
        _store_f_slice(g_s, g, s)

    @pl.when(jnp.logical_and(active, s >= FF_KA))
    def _():
        y_ref[...] = _dot(g_s[...], w2_ref[...].astype(BF16))

    @pl.when(jnp.logical_not(active))
    def _():
        y_ref[...] = jnp.zeros_like(y_ref)


def _moe(tile_exp, n_tiles, row_src, h2, w1, w3, w2):
    def up(i, s, te, nt, src):
        return (te[i], 0, jnp.where(i < nt[0], jnp.minimum(s, FF_KA - 1), FF_KA - 1))

    def down(i, s, te, nt, src):
        return (te[i], 0, jnp.where(i < nt[0], jnp.maximum(s - FF_KA, 0), MOE_KB - 1))

    return pl.pallas_call(
        _moe_body,
        grid_spec=pltpu.PrefetchScalarGridSpec(
            num_scalar_prefetch=3,
            grid=(MOE_TILES, FF_KA + MOE_KB),
            in_specs=[pl.BlockSpec(memory_space=pl.ANY),
                      pl.BlockSpec((None, D, FF_TF), up),
                      pl.BlockSpec((None, D, FF_TF), up),
                      pl.BlockSpec((None, D_FF, MOE_TN), down)],
            out_specs=pl.BlockSpec((MOE_TM, MOE_TN), lambda i, s, te, nt, src: (i, jnp.maximum(s - FF_KA, 0))),
            scratch_shapes=[pltpu.VMEM((MOE_GATHER, D), F32), pltpu.VMEM((MOE_TM, D), BF16),
                            pltpu.VMEM((MOE_TM, D_FF), BF16), pltpu.SemaphoreType.DMA(())],
        ),
        out_shape=jax.ShapeDtypeStruct((MOE_ROWS, D), F32),
        compiler_params=_cparams(("arbitrary", "arbitrary")),
        name="moe_experts",
    )(tile_exp, n_tiles, row_src, h2, w1, w3, w2)


def _combine_body(slot_ref, y_hbm, x_ref, g_ref, mod_ref, o_ref, buf, sem, *, tc):
    base = pl.program_id(0) * tc

    def row_copy(r, k, s):
        return pltpu.make_async_copy(y_hbm.at[pl.ds(s, 1)], buf.at[k, pl.ds(r, 1)], sem)

    def issue(r, c):
        for k in range(2):
            row_copy(r, k, slot_ref[2 * (base + r) + k]).start()
        return c

    lax.fori_loop(0, tc, issue, 0, unroll=8)
    for k in range(2):
        pltpu.make_async_copy(y_hbm.at[pl.ds(0, tc)], buf.at[k], sem).wait()
    g = g_ref[...]
    f = g[:, 0:1] * buf[0] + g[:, 1:2] * buf[1]
    o_ref[...] = x_ref[...] + mod_ref[5:6, :] * f


def _combine(slot, y, x1, gates, mod):
    tc = 256
    return pl.pallas_call(
        functools.partial(_combine_body, tc=tc),
        grid_spec=pltpu.PrefetchScalarGridSpec(
            num_scalar_prefetch=1,
            grid=(N_TOK // tc,),
            in_specs=[pl.BlockSpec(memory_space=pl.ANY),
                      pl.BlockSpec((tc, D), lambda i, s: (i, 0)),
                      pl.BlockSpec((tc, LANE), lambda i, s: (i, 0)),
                      pl.BlockSpec((None, 6, D), lambda i, s: (_mod_row(i, tc), 0, 0))],
            out_specs=pl.BlockSpec((tc, D), lambda i, s: (i, 0)),
            scratch_shapes=[pltpu.VMEM((2, tc, D), F32), pltpu.SemaphoreType.DMA(())],
        ),
        out_shape=jax.ShapeDtypeStruct((N_TOK, D), F32),
        compiler_params=_cparams(("arbitrary",)),
        name="moe_combine",
    )(slot, y, x1, gates, mod)


def _route_meta(idx2):
    flat = idx2.reshape(-1)
    oh = (flat[:, None] == jnp.arange(N_EXP, dtype=jnp.int32)[None, :]).astype(jnp.int32)
    csum = jnp.cumsum(oh, axis=0)
    counts = csum[-1]
    rank = jnp.sum((csum - oh) * oh, axis=1)
    tiles = (counts + MOE_TM - 1) // MOE_TM
    tile_end = jnp.cumsum(tiles)
    tile_start = tile_end - tiles
    n_tiles = tile_end[-1]
    slot = (tile_start[flat] * MOE_TM + rank).astype(jnp.int32)
    j = jnp.arange(MOE_TILES, dtype=jnp.int32)
    te = jnp.sum((j[:, None] >= tile_end[None, :]).astype(jnp.int32), axis=1)
    te = jnp.where(j < n_tiles, te, te[jnp.maximum(n_tiles - 1, 0)])
    te = jnp.minimum(te, N_EXP - 1).astype(jnp.int32)
    tok = jnp.arange(2 * N_TOK, dtype=jnp.int32) // 2
    row_src = jnp.zeros((MOE_ROWS,), jnp.int32).at[slot].set(tok, unique_indices=True)
    return te, n_tiles.reshape(1).astype(jnp.int32), row_src, slot


def _prep_w_in(w):
    s = np.cumsum([0, 256, 256, 512, 512, 32, 512, 1024, 256, 256])
    gq, gk, gv, gg, lr, ff, aq, ak, av = [w[:, s[i]:s[i + 1]] for i in range(9)]
    pad = jnp.zeros((D, Z_W - Z_LR - 2 * GLA_RANK), w.dtype)
    return jnp.concatenate([gq, gk, gv, gg, ff, aq, ak, av, lr, pad], axis=1).astype(BF16)


def _prep_w_lr(w_lr2):
    out = jnp.zeros((2, LANE, GLA_H * GLA_DK), F32)
    out = out.at[0, 0:GLA_RANK].set(w_lr2[0]).at[1, GLA_RANK:2 * GLA_RANK].set(w_lr2[1])
    return out.astype(BF16)


def kernel(x_prompt, x_sample, cache_k, cache_v, state_gla, c, c_ctx, w_ada, b_ada, norm1_w, norm2_w, w_in, w_gla_lr2, b_gla_lr2, gla_norm_w, q_norm_w, k_norm_w, attn_sink, w_out, ffn_w1, ffn_w3, ffn_w2, moe_router, moe_w1, moe_w3, moe_w2):
    x = (x_prompt.reshape(N_CTX, D), x_sample.reshape(N_LAT, D))
    cond16 = jnp.concatenate([c_ctx[None, :], c, jnp.zeros((16 - 1 - N_LAT_B, D), F32)], axis=0)
    mods = _mods(cond16, w_ada, b_ada).reshape(DEPTH, 16, 6, D)
    ck_all = cache_k.reshape(N_LAT_B, DEPTH, PAST, N_KV * HEAD_DIM)
    cv_all = cache_v.reshape(N_LAT_B, DEPTH, PAST, N_KV * HEAD_DIM)
    s0_all = jnp.swapaxes(state_gla, -1, -2)
    zero_state = jnp.zeros((N_CTX_B, 2, GLA_H, GLA_DV, GLA_DK), F32)
    ctx_units = dict(T=CTX_T, n_units=N_CTX_B, unit0=0)
    lat_units = dict(T=LAT_T, n_units=N_LAT_B, unit0=N_CTX // LAT_T)

    ks, vs, ss = [], [], []
    for l in range(DEPTH):
        mod = mods[l]
        z = _inproj(x, mod, norm1_w[l], _prep_w_in(w_in[l]))
        wlr = _prep_w_lr(w_gla_lr2[l])
        blr = b_gla_lr2[l].reshape(2, 1, GLA_H * GLA_DK)
        yg_c, st_c = _gla(z, wlr, blr, zero_state, gla_norm_w[l], **ctx_units)
        yg_l, _ = _gla(z, wlr, blr, s0_all[:, l], gla_norm_w[l], **lat_units)
        y_gla = (yg_c, yg_l)
        y_fft = (_fft(z, **ctx_units), _fft(z, **lat_units))
        ya_c, k_c, v_c = _attn(z, attn_sink[l], q_norm_w[l], k_norm_w[l], **ctx_units)
        ya_l = _attn(z, attn_sink[l], q_norm_w[l], k_norm_w[l], ctx_k=ck_all[:, l], ctx_v=cv_all[:, l],
                     **lat_units)
        y_att = (ya_c, ya_l)
        w_out_b = w_out[l].astype(BF16)
        e = l // 2
        if l % 2 == 0:
            x1, h2 = _outproj(y_gla, y_fft, y_att, w_out_b, x, mod, norm2_w[l])
            x = _ffn(h2, ffn_w1[e].astype(BF16), ffn_w3[e].astype(BF16), ffn_w2[e].astype(BF16), x1, mod)
        else:
            router_b = jnp.pad(moe_router[e], ((0, 0), (0, LANE - N_EXP))).astype(BF16)
            x1, h2, idx, gates = _outproj(y_gla, y_fft, y_att, w_out_b, x, mod, norm2_w[l], router_b)
            te, n_tiles, row_src, slot = _route_meta(idx[:, :2])
            y = _moe(te, n_tiles, row_src, h2, moe_w1[e], moe_w3[e], moe_w2[e])
            x = _combine(slot, y, x1, gates, mod)
        ks.append(k_c.reshape(N_CTX_B, CTX_T, N_KV, HEAD_DIM))
        vs.append(v_c.reshape(N_CTX_B, CTX_T, N_KV, HEAD_DIM))
        ss.append(jnp.swapaxes(st_c, -1, -2))
    y_prompt = x[:N_CTX].reshape(N_CTX_B, CTX_T, D)
    y_sample = x[N_CTX:].reshape(N_LAT_B, LAT_T, D)
    return (y_prompt, y_sample, jnp.stack(ks, axis=1), jnp.stack(vs, axis=1), jnp.stack(ss, axis=1))
```

```python
import functools

import numpy as np
import jax
import jax.numpy as jnp
from jax import lax
from jax.experimental import pallas as pl
from jax.experimental.pallas import tpu as pltpu

F32 = jnp.float32
BF16 = jnp.bfloat16

D = 2048
N_CTX_B, CTX_T = 32, 256
N_LAT_B, LAT_T = 8, 1024
N_CTX = N_CTX_B * CTX_T
N_LAT = N_LAT_B * LAT_T
N_TOK = N_CTX + N_LAT
DEPTH = 2
PAST = 512
GRID_W = 64
GLA_H, GLA_DK, GLA_DV, GLA_RANK, GLA_TAU, GLA_C = 4, 64, 128, 16, 16.0, 32
FFT_G, FFT_CH = 4, 128
N_HEADS, N_KV, HEAD_DIM = 8, 2, 128
GQA = N_HEADS // N_KV
WINDOW, ATT_BLOCK = 128, 128
ROPE_BASE = 10000.0
D_FF = 5632
N_EXP = 8
EPS = 1e-6
NEG_INF = -1e30

Z_GQ, Z_GK, Z_GV, Z_GG, Z_FF, Z_AQ, Z_AK, Z_AV, Z_LR = 0, 256, 512, 1024, 1536, 2048, 3072, 3328, 3584
Z_W = 3840
LANE = 128

VMEM_LIMIT = 56 * 1024 * 1024

MOE_TM = 1024
MOE_TILES = 2 * N_TOK // MOE_TM + N_EXP
MOE_ROWS = MOE_TILES * MOE_TM


def _cparams(sem):
    return pltpu.CompilerParams(dimension_semantics=sem, vmem_limit_bytes=VMEM_LIMIT)


def _silu(x):
    return x / (1.0 + jnp.exp(-x))


def _norm_mod(x, nw, scale, shift):
    ms = jnp.mean(x * x, axis=-1, keepdims=True)
    return (x * lax.rsqrt(ms + EPS) * nw) * (1.0 + scale) + shift


def _dot(a, b):
    return jnp.dot(a, b, preferred_element_type=F32)


def _dot_nt(a, b):
    return lax.dot_general(a, b, (((1,), (1,)), ((), ())), preferred_element_type=F32)


def _dot_tn(a, b):
    return lax.dot_general(a, b, (((0,), (0,)), ((), ())), preferred_element_type=F32)


def _mod_body(c_ref, w_ref, b_ref, o_ref):
    s = _silu(c_ref[...])
    o_ref[...] = _dot(s.astype(BF16), w_ref[...].astype(BF16)) + b_ref[...]


def _mods(cond16, w_ada, b_ada):
    tn = 1024
    return pl.pallas_call(
        _mod_body,
        grid=(DEPTH, 6 * D // tn),
        in_specs=[pl.BlockSpec((16, D), lambda l, j: (0, 0)),
                  pl.BlockSpec((None, D, tn), lambda l, j: (l, 0, j)),
                  pl.BlockSpec((None, 1, tn), lambda l, j: (l, 0, j))],
        out_specs=pl.BlockSpec((None, 16, tn), lambda l, j: (l, 0, j)),
        out_shape=jax.ShapeDtypeStruct((DEPTH, 16, 6 * D), F32),
        compiler_params=_cparams(("arbitrary", "arbitrary")),
        name="adaln_mod",
    )(cond16, w_ada, b_ada.reshape(DEPTH, 1, 6 * D))


def _mod_row(i, tm):
    per_lat = LAT_T // tm
    n_ctx_tiles = N_CTX // tm
    return jnp.where(i < n_ctx_tiles, 0, 1 + (i - n_ctx_tiles) // per_lat)


def _split_x(x, tm):
    if isinstance(x, (tuple, list)):
        return x[0], x[1], 0
    return x, x, N_CTX // tm


def _x_specs(tm, off_b, width):
    n_ctx_tiles = N_CTX // tm
    return [pl.BlockSpec((tm, width), lambda i: (jnp.minimum(i, n_ctx_tiles - 1), 0)),
            pl.BlockSpec((tm, width), lambda i: (jnp.maximum(i - n_ctx_tiles, 0) + off_b, 0))]


def _pick_x(xa_ref, xb_ref, tm):
    return jnp.where(pl.program_id(0) < N_CTX // tm, xa_ref[...], xb_ref[...])


def _inproj_body(xa_ref, xb_ref, mod_ref, n_ref, w_ref, o_ref, *, tm):
    h = _norm_mod(_pick_x(xa_ref, xb_ref, tm), n_ref[...], mod_ref[1:2, :], mod_ref[0:1, :])
    o_ref[...] = _dot(h.astype(BF16), w_ref[...]).astype(BF16)


def _inproj(x, mod, n1, w_in_b):
    tm = 512
    xa, xb, off_b = _split_x(x, tm)
    return pl.pallas_call(
        functools.partial(_inproj_body, tm=tm),
        grid=(N_TOK // tm,),
        in_specs=_x_specs(tm, off_b, D) + [
            pl.BlockSpec((None, 6, D), lambda i: (_mod_row(i, tm), 0, 0)),
            pl.BlockSpec((1, D), lambda i: (0, 0)),
            pl.BlockSpec((D, Z_W), lambda i: (0, 0), pipeline_mode=pl.Buffered(1))],
        out_specs=pl.BlockSpec((tm, Z_W), lambda i: (i, 0)),
        out_shape=jax.ShapeDtypeStruct((N_TOK, Z_W), BF16),
        compiler_params=_cparams(("arbitrary",)),
        name="inproj",
    )(xa, xb, mod, n1.reshape(1, D), w_in_b)


def _gla_body(qk_ref, v_ref, gg_ref, lr_ref, wlr_ref, blr_ref, s0_ref, nw_ref, y_ref, st_ref,
              g_s, o_s, st_s, qt_s, ke_s, dec_s, *, T):
    C = GLA_C
    NC = T // C
    qw = GLA_H * GLA_DK
    lr = lr_ref[...]
    for d in range(2):
        x = _dot(lr, wlr_ref[d]) + blr_ref[d]
        g_s[d] = (jnp.minimum(x, 0.0) - jnp.log1p(jnp.exp(-jnp.abs(x)))) * (1.0 / GLA_TAU)
    st_s[...] = s0_ref[...]

    R = 128
    cpb = R // C
    ri = lax.broadcasted_iota(jnp.int32, (R, R), 0)
    ci = lax.broadcasted_iota(jnp.int32, (R, R), 1)
    same = (ri // C) == (ci // C)
    masks = (jnp.logical_and(same, ci <= ri), jnp.logical_and(same, ci >= ri))
    sum_ops = tuple(jnp.concatenate([m.astype(BF16), same.astype(BF16)], axis=0) for m in masks)

    def intra(i, carry):
        r0 = pl.multiple_of(i * R, R)
        q = qk_ref[pl.ds(r0, R), 0:qw].astype(F32) * (GLA_DK ** -0.5)
        k = qk_ref[pl.ds(r0, R), qw:2 * qw].astype(F32)
        vb = v_ref[pl.ds(r0, R), :]
        heads = [(d, h) for d in range(2) for h in range(GLA_H)]
        ks = [slice(h * GLA_DK, (h + 1) * GLA_DK) for h in range(GLA_H)]
        sums = []
        for d in range(2):
            g = g_s[d, pl.ds(r0, R), :]
            g_hi = g.astype(BF16)
            r1 = g - g_hi.astype(F32)
            g_mid = r1.astype(BF16)
            g_lo = (r1 - g_mid.astype(F32)).astype(BF16)
            sums.append(_dot(sum_ops[d], jnp.concatenate([g_hi, g_mid, g_lo], axis=1)))
        qts, kts = [], []
        for d in range(2):
            s3 = sums[d][:, 0:qw] + sums[d][:, qw:2 * qw] + sums[d][:, 2 * qw:3 * qw]
            b = s3[0:R]
            bl = s3[R:2 * R]
            qts.append((q * jnp.exp(b)).astype(BF16))
            kts.append((k * jnp.exp(-b)).astype(BF16))
            qt_s[d, pl.ds(r0, R), :] = qts[d]
            ke_s[d, pl.ds(r0, R), :] = (k * jnp.exp(bl - b)).astype(BF16)
            ebl = jnp.exp(bl)
            for j in range(cpb):
                dec_s[d, pl.ds(i * cpb + j, 1), :] = ebl[j * C:j * C + 1, :]
        a = [_dot_nt(qts[d][:, ks[h]], kts[d][:, ks[h]]) for d, h in heads]
        a = [jnp.where(masks[d], a[n], 0.0).astype(BF16) for n, (d, h) in enumerate(heads)]
        o = [_dot(a[n], vb[:, h * GLA_DV:(h + 1) * GLA_DV]) for n, (d, h) in enumerate(heads)]
        for d in range(2):
            o_s[d, pl.ds(r0, R), :] = jnp.concatenate(o[d * GLA_H:(d + 1) * GLA_H], axis=1)
        return carry

    lax.fori_loop(0, T // R, intra, 0)

    def inter(i, carry):
        heads = [(d, h) for d in range(2) for h in range(GLA_H)]
        ks = [slice(h * GLA_DK, (h + 1) * GLA_DK) for h in range(GLA_H)]
        cs = (i, NC - 1 - i)
        r0 = [pl.multiple_of(c * C, C) for c in cs]
        qt = [qt_s[d, pl.ds(r0[d], C), :] for d in range(2)]
        ke = [ke_s[d, pl.ds(r0[d], C), :] for d in range(2)]
        vb = [v_ref[pl.ds(r0[d], C), :] for d in range(2)]
        dec = [dec_s[d, pl.ds(cs[d], 1), :] for d in range(2)]
        o = []
        for d, h in heads:
            s_t = st_s[d, h]
            o.append(_dot_nt(qt[d][:, ks[h]], s_t.astype(BF16)))
            u_t = _dot_tn(vb[d][:, h * GLA_DV:(h + 1) * GLA_DV], ke[d][:, ks[h]])
            st_s[d, h] = s_t * dec[d][:, ks[h]] + u_t
        for d in range(2):
            o_s[d, pl.ds(r0[d], C), :] += jnp.concatenate(o[d * GLA_H:(d + 1) * GLA_H], axis=1)
        return carry

    lax.fori_loop(0, NC, inter, 0)

    ch = 128
    nw = nw_ref[...]

    def epi(i, carry):
        r = pl.multiple_of(i * ch, ch)
        for h in range(GLA_H):
            cs = slice(h * GLA_DV, (h + 1) * GLA_DV)
            o = o_s[0, pl.ds(r, ch), cs] + o_s[1, pl.ds(r, ch), cs]
            ms = jnp.mean(o * o, axis=-1, keepdims=True)
            y = (o * lax.rsqrt(ms + EPS) * nw) * _silu(gg_ref[pl.ds(r, ch), cs].astype(F32))
            y_ref[pl.ds(r, ch), cs] = y.astype(BF16)
        return carry

    lax.fori_loop(0, T // ch, epi, 0)
    st_ref[...] = st_s[...]


def _gla(z, wlr, blr, s0_t, nw, *, T, n_units, unit0):
    rb = lambda u: u + unit0
    hv = GLA_H * GLA_DV
    qw = GLA_H * GLA_DK
    in_specs = [pl.BlockSpec((T, 2 * qw), lambda u: (rb(u), Z_GQ // (2 * qw))),
                pl.BlockSpec((T, hv), lambda u: (rb(u), Z_GV // hv)),
                pl.BlockSpec((T, hv), lambda u: (rb(u), Z_GG // hv)),
                pl.BlockSpec((T, LANE), lambda u: (rb(u), Z_LR // LANE)),
                pl.BlockSpec((2, LANE, qw), lambda u: (0, 0, 0)),
                pl.BlockSpec((2, 1, qw), lambda u: (0, 0, 0)),
                pl.BlockSpec((None, 2, GLA_H, GLA_DV, GLA_DK), lambda u: (u, 0, 0, 0, 0)),
                pl.BlockSpec((1, GLA_DV), lambda u: (0, 0))]
    args = [z, z, z, z, wlr, blr, s0_t, nw.reshape(1, GLA_DV)]
    return pl.pallas_call(
        functools.partial(_gla_body, T=T),
        grid=(n_units,),
        in_specs=in_specs,
        out_specs=[pl.BlockSpec((T, hv), lambda u: (u, 0)),
                   pl.BlockSpec((None, 2, GLA_H, GLA_DV, GLA_DK), lambda u: (u, 0, 0, 0, 0))],
        out_shape=[jax.ShapeDtypeStruct((n_units * T, hv), BF16),
                   jax.ShapeDtypeStruct((n_units, 2, GLA_H, GLA_DV, GLA_DK), F32)],
        scratch_shapes=[pltpu.VMEM((2, T, qw), F32),
                        pltpu.VMEM((2, T, hv), F32),
                        pltpu.VMEM((2, GLA_H, GLA_DV, GLA_DK), F32),
                        pltpu.VMEM((2, T, qw), BF16),
                        pltpu.VMEM((2, T, qw), BF16),
                        pltpu.VMEM((2, T // GLA_C, qw), F32)],
        compiler_params=_cparams(("arbitrary",)),
        name=f"gla_T{T}",
    )(*args)


def _fft_body(x_ref, w2_ref, dt_ref, y_ref, p_s, *, T):
    xb = x_ref[...]
    for g in range(FFT_G):
        cs = slice(g * FFT_CH, (g + 1) * FFT_CH)
        p = _dot(xb[:, cs], w2_ref[...])
        p_s[0:T, cs] = p[:, :FFT_CH].astype(BF16)
        p_s[T:2 * T, cs] = p[:, FFT_CH:].astype(BF16)
    y_ref[...] = _dot(dt_ref[...], p_s[...]).astype(BF16)


def _dft_consts(T):
    c = np.arange(FFT_CH)
    ang_c = (np.outer(c, c) % FFT_CH) * (2.0 * np.pi / FFT_CH)
    w2 = np.concatenate([np.cos(ang_c), np.sin(ang_c)], axis=1) / np.sqrt(FFT_CH)
    t = np.arange(T)
    ang_t = (np.outer(t, t) % T) * (2.0 * np.pi / T)
    dt = np.concatenate([np.cos(ang_t), -np.sin(ang_t)], axis=1) / np.sqrt(T)
    return jnp.asarray(w2, F32).astype(BF16), jnp.asarray(dt, F32).astype(BF16)


def _fft(z, *, T, n_units, unit0):
    w2, dt = _dft_consts(T)
    fw = FFT_G * FFT_CH
    in_specs = [pl.BlockSpec((T, fw), lambda u: (u + unit0, Z_FF // fw)),
                pl.BlockSpec((FFT_CH, 2 * FFT_CH), lambda u: (0, 0)),
                pl.BlockSpec((T, 2 * T), lambda u: (0, 0))]
    args = [z, w2, dt]
    return pl.pallas_call(
        functools.partial(_fft_body, T=T),
        grid=(n_units,),
        in_specs=in_specs,
        out_specs=pl.BlockSpec((T, fw), lambda u: (u, 0)),
        out_shape=jax.ShapeDtypeStruct((n_units * T, fw), BF16),
        scratch_shapes=[pltpu.VMEM((2 * T, fw), BF16)],
        compiler_params=_cparams(("arbitrary",)),
        name=f"fft_T{T}",
    )(*args)


def _rope_tables(T):
    half = HEAD_DIM // 2
    inv = ROPE_BASE ** (-np.arange(0, half, 2, dtype=np.float64) / half)
    t = np.arange(T)
    ang_r = (t // GRID_W)[:, None] * inv[None, :]
    ang_c = (t % GRID_W)[:, None] * inv[None, :]
    cos = np.concatenate([np.cos(ang_r)] * 2 + [np.cos(ang_c)] * 2, axis=1)
    sin = np.concatenate([-np.sin(ang_r), np.sin(ang_r), -np.sin(ang_c), np.sin(ang_c)], axis=1)
    return jnp.asarray(cos, F32), jnp.asarray(sin, F32)


def _rope(x, cos, sin):
    lane = lax.broadcasted_iota(jnp.int32, x.shape, 1)
    quarter = HEAD_DIM // 4
    partner = jnp.where(lane % (2 * quarter) < quarter,
                        pltpu.roll(x, HEAD_DIM - quarter, 1), pltpu.roll(x, quarter, 1))
    return x * cos + partner * sin


def _rms128(x, w):
    return x * lax.rsqrt(jnp.mean(x * x, axis=-1, keepdims=True) + EPS) * w


def _attn_body(*refs, T, latent):
    if latent:
        (sink_ref, q_ref, k_ref, v_ref, qw_ref, kw_ref, ck_ref, cv_ref, cos_ref, sin_ref, y_ref) = refs
    else:
        (sink_ref, q_ref, k_ref, v_ref, qw_ref, kw_ref, y_ref, ko_ref, vo_ref) = refs
    kv = pl.program_id(1)
    nb = T // ATT_BLOCK
    kn = _rms128(k_ref[...].astype(F32), kw_ref[...])
    vb = v_ref[...]
    if latent:
        kn = _rope(kn, cos_ref[...], sin_ref[...])
        ckb = ck_ref[...].astype(BF16)
        cvb = cv_ref[...].astype(BF16)
    else:
        ko_ref[...] = kn
        vo_ref[...] = vb.astype(F32)
    kb = kn.astype(BF16)
    rows_g = lax.broadcasted_iota(jnp.int32, (GQA * ATT_BLOCK, 1), 0) // ATT_BLOCK
    sink = jnp.zeros((GQA * ATT_BLOCK, 1), F32)
    for g in range(GQA):
        sink = jnp.where(rows_g == g, sink_ref[kv * GQA + g], sink)
    scale = HEAD_DIM ** -0.5
    for n in range(nb):
        rs = slice(n * ATT_BLOCK, (n + 1) * ATT_BLOCK)
        qs = []
        for g in range(GQA):
            qn = _rms128(q_ref[rs, g * HEAD_DIM:(g + 1) * HEAD_DIM].astype(F32), qw_ref[...])
            if latent:
                qn = _rope(qn, cos_ref[rs, :], sin_ref[rs, :])
            qs.append((qn * scale).astype(BF16))
        qq = jnp.concatenate(qs, axis=0)
        if latent:
            lo, hi = max(n - 1, 0), min(n + 2, nb)
            ks = slice(lo * ATT_BLOCK, hi * ATT_BLOCK)
            w = (hi - lo) * ATT_BLOCK
            s_loc = _dot_nt(qq, kb[ks])
            qpos = n * ATT_BLOCK + lax.broadcasted_iota(jnp.int32, (GQA * ATT_BLOCK, w), 0) % ATT_BLOCK
            kpos = lo * ATT_BLOCK + lax.broadcasted_iota(jnp.int32, (GQA * ATT_BLOCK, w), 1)
            s_loc = jnp.where(jnp.abs(kpos - qpos) <= WINDOW, s_loc, NEG_INF)
            s_ctx = _dot_nt(qq, ckb)
            m = jnp.maximum(jnp.maximum(jnp.max(s_loc, axis=-1, keepdims=True),
                                        jnp.max(s_ctx, axis=-1, keepdims=True)), sink)
            p_loc = jnp.exp(s_loc - m)
            p_ctx = jnp.exp(s_ctx - m)
            den = (jnp.sum(p_loc, axis=-1, keepdims=True) + jnp.sum(p_ctx, axis=-1, keepdims=True)
                   + jnp.exp(sink - m))
            o = _dot(p_loc.astype(BF16), vb[ks]) + _dot(p_ctx.astype(BF16), cvb)
        else:
            s = _dot_nt(qq, kb)
            m = jnp.maximum(jnp.max(s, axis=-1, keepdims=True), sink)
            p = jnp.exp(s - m)
            den = jnp.sum(p, axis=-1, keepdims=True) + jnp.exp(sink - m)
            o = _dot(p.astype(BF16), vb)
        o = o * (1.0 / den)
        for g in range(GQA):
            y_ref[rs, g * HEAD_DIM:(g + 1) * HEAD_DIM] = o[g * ATT_BLOCK:(g + 1) * ATT_BLOCK].astype(BF16)


def _attn(z, sink_l, qw, kw, *, T, n_units, unit0, ctx_k=None, ctx_v=None):
    latent = ctx_k is not None
    qwid = GQA * HEAD_DIM
    rb = lambda u: u + unit0
    smem = pl.BlockSpec(memory_space=pltpu.SMEM)
    in_specs = [smem,
                pl.BlockSpec((T, qwid), lambda u, h: (rb(u), Z_AQ // qwid + h)),
                pl.BlockSpec((T, HEAD_DIM), lambda u, h: (rb(u), Z_AK // HEAD_DIM + h)),
                pl.BlockSpec((T, HEAD_DIM), lambda u, h: (rb(u), Z_AV // HEAD_DIM + h)),
                pl.BlockSpec((1, HEAD_DIM), lambda u, h: (0, 0)),
                pl.BlockSpec((1, HEAD_DIM), lambda u, h: (0, 0))]
    args = [sink_l, z, z, z, qw.reshape(1, HEAD_DIM), kw.reshape(1, HEAD_DIM)]
    y_spec = pl.BlockSpec((T, qwid), lambda u, h: (u, h))
    y_shape = jax.ShapeDtypeStruct((n_units * T, N_HEADS * HEAD_DIM), BF16)
    if latent:
        cos, sin = _rope_tables(T)
        in_specs += [pl.BlockSpec((None, PAST, HEAD_DIM), lambda u, h: (u, 0, h)),
                     pl.BlockSpec((None, PAST, HEAD_DIM), lambda u, h: (u, 0, h)),
                     pl.BlockSpec((T, HEAD_DIM), lambda u, h: (0, 0)),
                     pl.BlockSpec((T, HEAD_DIM), lambda u, h: (0, 0))]
        args += [ctx_k, ctx_v, cos, sin]
        out_specs, out_shape = y_spec, y_shape
    else:
        kv_spec = pl.BlockSpec((T, HEAD_DIM), lambda u, h: (u, h))
        kv_shape = jax.ShapeDtypeStruct((n_units * T, N_KV * HEAD_DIM), F32)
        out_specs, out_shape = [y_spec, kv_spec, kv_spec], [y_shape, kv_shape, kv_shape]
    return pl.pallas_call(
        functools.partial(_attn_body, T=T, latent=latent),
        grid=(n_units, N_KV),
        in_specs=in_specs,
        out_specs=out_specs,
        out_shape=out_shape,
        compiler_params=_cparams(("arbitrary", "arbitrary")),
        name=f"attn_T{T}",
    )(*args)


def _outproj_body(*refs, router, tm):
    y_refs, refs = refs[:6], refs[6:]
    if router:
        (w_ref, xa_ref, xb_ref, mod_ref, n2_ref, r_ref, x1_ref, h2_ref, idx_ref, gate_ref, cnt_ref, cnt_s) = refs
    else:
        (w_ref, xa_ref, xb_ref, mod_ref, n2_ref, x1_ref, h2_ref) = refs
    gw = GLA_H * GLA_DV
    fw = FFT_G * FFT_CH
    yg, yf, ya = (_pick_x(y_refs[2 * j], y_refs[2 * j + 1], tm) for j in range(3))
    acc = _dot(yg, w_ref[0:gw, :]) + _dot(yf, w_ref[gw:gw + fw, :]) + _dot(ya, w_ref[gw + fw:, :])
    x1 = _pick_x(xa_ref, xb_ref, tm) + mod_ref[2:3, :] * acc
    x1_ref[...] = x1
    h2 = _norm_mod(x1, n2_ref[...], mod_ref[4:5, :], mod_ref[3:4, :])
    h2_ref[...] = h2.astype(h2_ref.dtype)
    if router:
        logits = _dot(h2.astype(BF16), r_ref[...])
        lane = lax.broadcasted_iota(jnp.int32, logits.shape, 1)
        lg = jnp.where(lane < N_EXP, logits, -jnp.inf)
        v1 = jnp.max(lg, axis=-1, keepdims=True)
        i1 = jnp.min(jnp.where(lg == v1, lane, LANE), axis=-1, keepdims=True)
        lg2 = jnp.where(lane == i1, -jnp.inf, lg)
        v2 = jnp.max(lg2, axis=-1, keepdims=True)
        i2 = jnp.min(jnp.where(lg2 == v2, lane, LANE), axis=-1, keepdims=True)
        e = jnp.exp(v2 - v1)
        g1 = 1.0 / (1.0 + e)
        gate_ref[...] = jnp.where(lane == 0, g1, jnp.where(lane == 1, e * g1, 0.0))

        @pl.when(pl.program_id(0) == 0)
        def _():
            cnt_s[...] = jnp.zeros_like(cnt_s)

        oh1 = lane == i1
        oh2 = lane == i2
        oh = jnp.where(jnp.logical_or(oh1, oh2), 1.0, 0.0)
        ri = lax.broadcasted_iota(jnp.int32, (tm, tm), 0)
        ci = lax.broadcasted_iota(jnp.int32, (tm, tm), 1)
        before = jnp.where(ci < ri, 1.0, 0.0).astype(BF16)
        prior = _dot(before, oh.astype(BF16)) + cnt_s[0:1, :]
        r1 = jnp.sum(jnp.where(oh1, prior, 0.0), axis=-1, keepdims=True).astype(jnp.int32)
        r2 = jnp.sum(jnp.where(oh2, prior, 0.0), axis=-1, keepdims=True).astype(jnp.int32)
        idx_ref[...] = jnp.where(lane == 0, i1, jnp.where(lane == 1, i2,
                                 jnp.where(lane == 2, r1, jnp.where(lane == 3, r2, 0))))
        cnt_s[...] = cnt_s[...] + jnp.sum(oh, axis=0, keepdims=True)
        cnt_ref[...] = cnt_s[...]


def _outproj(y_gla, y_fft, y_att, w_out_b, x, mod, n2, router_b=None):
    tm = 512
    router = router_b is not None
    gw, fw, aw = GLA_H * GLA_DV, FFT_G * FFT_CH, N_HEADS * HEAD_DIM
    row = lambda i: (i, 0)
    xa, xb, off_b = _split_x(x, tm)
    in_specs, args = [], []
    for pair, wid in ((y_gla, gw), (y_fft, fw), (y_att, aw)):
        in_specs += _x_specs(tm, 0, wid)
        args += list(pair)
    in_specs.append(pl.BlockSpec((gw + fw + aw, D), lambda i: (0, 0), pipeline_mode=pl.Buffered(1)))
    in_specs += _x_specs(tm, off_b, D)
    in_specs += [pl.BlockSpec((None, 6, D), lambda i: (_mod_row(i, tm), 0, 0)),
                 pl.BlockSpec((1, D), lambda i: (0, 0))]
    args += [w_out_b, xa, xb, mod, n2.reshape(1, D)]
    out_specs = [pl.BlockSpec((tm, D), row), pl.BlockSpec((tm, D), row)]
    out_shape = [jax.ShapeDtypeStruct((N_TOK, D), F32),
                 jax.ShapeDtypeStruct((N_TOK, D), F32 if router else BF16)]
    if router:
        in_specs.append(pl.BlockSpec((D, LANE), lambda i: (0, 0)))
        args.append(router_b)
        out_specs += [pl.BlockSpec((tm, LANE), row), pl.BlockSpec((tm, LANE), row),
                      pl.BlockSpec((8, LANE), lambda i: (0, 0))]
        out_shape += [jax.ShapeDtypeStruct((N_TOK, LANE), jnp.int32), jax.ShapeDtypeStruct((N_TOK, LANE), F32),
                      jax.ShapeDtypeStruct((8, LANE), F32)]
    return pl.pallas_call(
        functools.partial(_outproj_body, router=router, tm=tm),
        grid=(N_TOK // tm,),
        in_specs=in_specs,
        out_specs=out_specs,
        out_shape=out_shape,
        scratch_shapes=[pltpu.VMEM((8, LANE), F32)] if router else [],
        compiler_params=_cparams(("arbitrary",)),
        name="outproj_router" if router else "outproj",
    )(*args)


FF_TF = 512
FF_KA = D_FF // FF_TF


def _store_f_slice(g_s, g, s, rows=slice(None)):
    for kk in range(FF_KA):
        @pl.when(s == kk)
        def _(kk=kk):
            g_s[rows, kk * FF_TF:(kk + 1) * FF_TF] = g


def _ffn_body(h_ref, w1_ref, w3_ref, w2_ref, x_ref, mod_ref, o_ref, g_s):
    s = pl.program_id(1)

    @pl.when(s < FF_KA)
    def _():
        h = h_ref[...]
        g = (_silu(_dot(h, w1_ref[...])) * _dot(h, w3_ref[...])).astype(BF16)
        _store_f_slice(g_s, g, s)

    @pl.when(s >= FF_KA)
    def _():
        o_ref[...] = x_ref[...] + mod_ref[5:6, :] * _dot(g_s[...], w2_ref[...])


def _ffn(h2, w1_b, w3_b, w2_b, x1, mod):
    tm, tn = 1024, 512
    kb = D // tn
    up = lambda i, s: (0, jnp.minimum(s, FF_KA - 1))
    down = lambda s: jnp.maximum(s - FF_KA, 0)
    return pl.pallas_call(
        _ffn_body,
        grid=(N_TOK // tm, FF_KA + kb),
        in_specs=[pl.BlockSpec((tm, D), lambda i, s: (i, 0)),
                  pl.BlockSpec((D, FF_TF), up),
                  pl.BlockSpec((D, FF_TF), up),
                  pl.BlockSpec((D_FF, tn), lambda i, s: (0, down(s))),
                  pl.BlockSpec((tm, tn), lambda i, s: (i, down(s))),
                  pl.BlockSpec((None, 6, tn), lambda i, s: (_mod_row(i, tm), 0, down(s)))],
        out_specs=pl.BlockSpec((tm, tn), lambda i, s: (i, down(s))),
        out_shape=jax.ShapeDtypeStruct((N_TOK, D), F32),
        scratch_shapes=[pltpu.VMEM((tm, D_FF), BF16)],
        compiler_params=_cparams(("arbitrary", "arbitrary")),
        name="ffn_dense",
    )(h2, w1_b, w3_b, w2_b, x1, mod)


MOE_TN = 256
MOE_KB = D // MOE_TN
MOE_GATHER = 512


def _moe_body(te_ref, tr_ref, src_ref, h_hbm, w1_ref, w3_ref, w2_ref, y_ref, xg_s, xb_s, g_s, sem):
    i = pl.program_id(0)
    s = pl.program_id(1)
    n_rows = tr_ref[i]
    lower = slice(0, MOE_GATHER)
    cases = ((slice(None), n_rows > MOE_GATHER),
             (lower, jnp.logical_and(n_rows > 0, n_rows <= MOE_GATHER)))

    def row_copy(r, t):
        return pltpu.make_async_copy(h_hbm.at[pl.ds(t, 1)], xg_s.at[pl.ds(r, 1)], sem)

    for p in range(MOE_TM // MOE_GATHER):
        @pl.when(jnp.logical_and(n_rows > p * MOE_GATHER, s == 0))
        def _(p=p):
            base = i * MOE_TM + p * MOE_GATHER

            def issue(r, c):
                row_copy(r, src_ref[base + r]).start()
                return c

            lax.fori_loop(0, MOE_GATHER, issue, 0, unroll=8)
            pltpu.make_async_copy(h_hbm.at[pl.ds(0, MOE_GATHER)], xg_s, sem).wait()
            xb_s[p * MOE_GATHER:(p + 1) * MOE_GATHER, :] = xg_s[...].astype(BF16)

    for rows, cond in cases:
        @pl.when(jnp.logical_and(cond, s < FF_KA))
        def _(rows=rows):
            h = xb_s[rows, :]
            g = (_silu(_dot(h, w1_ref[...].astype(BF16))) * _dot(h, w3_ref[...].astype(BF16))).astype(BF16)
            _store_f_slice(g_s, g, s, rows)

        @pl.when(jnp.logical_and(cond, s >= FF_KA))
        def _(rows=rows):
            y_ref[rows, :] = _dot(g_s[rows, :], w2_ref[...].astype(BF16))

    @pl.when(n_rows <= MOE_GATHER)
    def _():
        y_ref[MOE_GATHER:, :] = jnp.zeros((MOE_TM - MOE_GATHER, MOE_TN), F32)

    @pl.when(n_rows == 0)
    def _():
        y_ref[lower, :] = jnp.zeros((MOE_GATHER, MOE_TN), F32)


def _moe(tile_exp, tile_rows, row_src, h2, w1, w3, w2):
    def up(i, s, te, tr, src):
        return (te[i], 0, jnp.where(tr[i] > 0, jnp.minimum(s, FF_KA - 1), FF_KA - 1))

    def down(i, s, te, tr, src):
        return (te[i], 0, jnp.where(tr[i] > 0, jnp.maximum(s - FF_KA, 0), MOE_KB - 1))

    return pl.pallas_call(
        _moe_body,
        grid_spec=pltpu.PrefetchScalarGridSpec(
            num_scalar_prefetch=3,
            grid=(MOE_TILES, FF_KA + MOE_KB),
            in_specs=[pl.BlockSpec(memory_space=pl.ANY),
                      pl.BlockSpec((None, D, FF_TF), up),
                      pl.BlockSpec((None, D, FF_TF), up),
                      pl.BlockSpec((None, D_FF, MOE_TN), down)],
            out_specs=pl.BlockSpec((MOE_TM, MOE_TN), lambda i, s, te, tr, src: (i, jnp.maximum(s - FF_KA, 0))),
            scratch_shapes=[pltpu.VMEM((MOE_GATHER, D), F32), pltpu.VMEM((MOE_TM, D), BF16),
                            pltpu.VMEM((MOE_TM, D_FF), BF16), pltpu.SemaphoreType.DMA(())],
        ),
        out_shape=jax.ShapeDtypeStruct((MOE_ROWS, D), F32),
        compiler_params=_cparams(("arbitrary", "arbitrary")),
        name="moe_experts",
    )(tile_exp, tile_rows, row_src, h2, w1, w3, w2)


def _combine_body(slot_ref, y_hbm, x_ref, g_ref, mod_ref, oc_ref, ol_ref, buf, sem, *, tc):
    base = pl.program_id(0) * tc

    def row_copy(r, k, s):
        return pltpu.make_async_copy(y_hbm.at[pl.ds(s, 1)], buf.at[k, pl.ds(r, 1)], sem)

    def issue(r, c):
        for k in range(2):
            row_copy(r, k, slot_ref[2 * (base + r) + k]).start()
        return c

    lax.fori_loop(0, tc, issue, 0, unroll=8)
    for k in range(2):
        pltpu.make_async_copy(y_hbm.at[pl.ds(0, tc)], buf.at[k], sem).wait()
    g = g_ref[...]
    f = g[:, 0:1] * buf[0] + g[:, 1:2] * buf[1]
    out = x_ref[...] + mod_ref[5:6, :] * f
    is_ctx = pl.program_id(0) < N_CTX // tc

    @pl.when(is_ctx)
    def _():
        oc_ref[...] = out

    @pl.when(jnp.logical_not(is_ctx))
    def _():
        ol_ref[...] = out


def _combine(slot, y, x1, gates, mod):
    tc = 256
    n_ctx_tiles = N_CTX // tc
    return pl.pallas_call(
        functools.partial(_combine_body, tc=tc),
        grid_spec=pltpu.PrefetchScalarGridSpec(
            num_scalar_prefetch=1,
            grid=(N_TOK // tc,),
            in_specs=[pl.BlockSpec(memory_space=pl.ANY),
                      pl.BlockSpec((tc, D), lambda i, s: (i, 0)),
                      pl.BlockSpec((tc, LANE), lambda i, s: (i, 0)),
                      pl.BlockSpec((None, 6, D), lambda i, s: (_mod_row(i, tc), 0, 0))],
            out_specs=[pl.BlockSpec((tc, D), lambda i, s: (jnp.minimum(i, n_ctx_tiles - 1), 0)),
                       pl.BlockSpec((tc, D), lambda i, s: (jnp.maximum(i - n_ctx_tiles, 0), 0))],
            scratch_shapes=[pltpu.VMEM((2, tc, D), F32), pltpu.SemaphoreType.DMA(())],
        ),
        out_shape=[jax.ShapeDtypeStruct((N_CTX, D), F32), jax.ShapeDtypeStruct((N_LAT, D), F32)],
        compiler_params=_cparams(("arbitrary",)),
        name="moe_combine",
    )(slot, y, x1, gates, mod)


def _route_meta(idx4, counts):
    counts = counts.astype(jnp.int32)
    tiles = (counts + MOE_TM - 1) // MOE_TM
    tile_end = jnp.cumsum(tiles)
    tile_start = tile_end - tiles
    n_tiles = tile_end[-1]
    exp_of = idx4[:, 0:2].reshape(-1)
    rank = idx4[:, 2:4].reshape(-1)
    first_row = jnp.sum(jnp.where(exp_of[:, None] == jnp.arange(N_EXP, dtype=jnp.int32)[None, :],
                                  (tile_start * MOE_TM)[None, :], 0), axis=1)
    slot = (first_row + rank).astype(jnp.int32)
    j = jnp.arange(MOE_TILES, dtype=jnp.int32)
    te = jnp.sum((j[:, None] >= tile_end[None, :]).astype(jnp.int32), axis=1)
    te = jnp.where(j < n_tiles, te, te[jnp.maximum(n_tiles - 1, 0)])
    te = jnp.minimum(te, N_EXP - 1).astype(jnp.int32)
    rows = jnp.clip(counts[te] - (j - tile_start[te]) * MOE_TM, 0, MOE_TM)
    rows = jnp.where(j < n_tiles, rows, 0).astype(jnp.int32)
    tok = jnp.arange(2 * N_TOK, dtype=jnp.int32) // 2
    row_src = jnp.zeros((MOE_ROWS,), jnp.int32).at[slot].set(tok, unique_indices=True)
    return te, rows, row_src, slot


def _prep_w_in(w):
    s = np.cumsum([0, 256, 256, 512, 512, 32, 512, 1024, 256, 256])
    gq, gk, gv, gg, lr, ff, aq, ak, av = [w[:, s[i]:s[i + 1]] for i in range(9)]
    pad = jnp.zeros((D, Z_W - Z_LR - 2 * GLA_RANK), w.dtype)
    return jnp.concatenate([gq, gk, gv, gg, ff, aq, ak, av, lr, pad], axis=1).astype(BF16)


def _prep_w_lr(w_lr2):
    out = jnp.zeros((2, LANE, GLA_H * GLA_DK), F32)
    out = out.at[0, 0:GLA_RANK].set(w_lr2[0]).at[1, GLA_RANK:2 * GLA_RANK].set(w_lr2[1])
    return out.astype(BF16)


def kernel(x_prompt, x_sample, cache_k, cache_v, state_gla, c, c_ctx, w_ada, b_ada, norm1_w, norm2_w, w_in, w_gla_lr2, b_gla_lr2, gla_norm_w, q_norm_w, k_norm_w, attn_sink, w_out, ffn_w1, ffn_w3, ffn_w2, moe_router, moe_w1, moe_w3, moe_w2):
    x = (x_prompt.reshape(N_CTX, D), x_sample.reshape(N_LAT, D))
    cond16 = jnp.concatenate([c_ctx[None, :], c, jnp.zeros((16 - 1 - N_LAT_B, D), F32)], axis=0)
    mods = _mods(cond16, w_ada, b_ada).reshape(DEPTH, 16, 6, D)
    ck_all = cache_k.reshape(N_LAT_B, DEPTH, PAST, N_KV * HEAD_DIM)
    cv_all = cache_v.reshape(N_LAT_B, DEPTH, PAST, N_KV * HEAD_DIM)
    s0_all = jnp.swapaxes(state_gla, -1, -2)
    zero_state = jnp.zeros((N_CTX_B, 2, GLA_H, GLA_DV, GLA_DK), F32)
    ctx_units = dict(T=CTX_T, n_units=N_CTX_B, unit0=0)
    lat_units = dict(T=LAT_T, n_units=N_LAT_B, unit0=N_CTX // LAT_T)

    ks, vs, ss = [], [], []
    for l in range(DEPTH):
        mod = mods[l]
        z = _inproj(x, mod, norm1_w[l], _prep_w_in(w_in[l]))
        wlr = _prep_w_lr(w_gla_lr2[l])
        blr = b_gla_lr2[l].reshape(2, 1, GLA_H * GLA_DK)
        yg_c, st_c = _gla(z, wlr, blr, zero_state, gla_norm_w[l], **ctx_units)
        yg_l, _ = _gla(z, wlr, blr, s0_all[:, l], gla_norm_w[l], **lat_units)
        y_gla = (yg_c, yg_l)
        y_fft = (_fft(z, **ctx_units), _fft(z, **lat_units))
        ya_c, k_c, v_c = _attn(z, attn_sink[l], q_norm_w[l], k_norm_w[l], **ctx_units)
        ya_l = _attn(z, attn_sink[l], q_norm_w[l], k_norm_w[l], ctx_k=ck_all[:, l], ctx_v=cv_all[:, l],
                     **lat_units)
        y_att = (ya_c, ya_l)
        w_out_b = w_out[l].astype(BF16)
        e = l // 2
        if l % 2 == 0:
            x1, h2 = _outproj(y_gla, y_fft, y_att, w_out_b, x, mod, norm2_w[l])
            x = _ffn(h2, ffn_w1[e].astype(BF16), ffn_w3[e].astype(BF16), ffn_w2[e].astype(BF16), x1, mod)
        else:
            router_b = jnp.pad(moe_router[e], ((0, 0), (0, LANE - N_EXP))).astype(BF16)
            x1, h2, idx, gates, counts = _outproj(y_gla, y_fft, y_att, w_out_b, x, mod, norm2_w[l], router_b)
            te, tile_rows, row_src, slot = _route_meta(idx[:, :4], counts[0, :N_EXP])
            y = _moe(te, tile_rows, row_src, h2, moe_w1[e], moe_w3[e], moe_w2[e])
            x = _combine(slot, y, x1, gates, mod)
        ks.append(k_c.reshape(N_CTX_B, CTX_T, N_KV, HEAD_DIM))
        vs.append(v_c.reshape(N_CTX_B, CTX_T, N_KV, HEAD_DIM))
        ss.append(jnp.swapaxes(st_c, -1, -2))
    if not isinstance(x, (tuple, list)):
        x = (x[:N_CTX], x[N_CTX:])
    y_prompt = x[0].reshape(N_CTX_B, CTX_T, D)
    y_sample = x[1].reshape(N_LAT_B, LAT_T, D)
    return (y_prompt, y_sample, jnp.stack(ks, axis=1), jnp.stack(vs, axis=1), jnp.stack(ss, axis=1))
```

```python
import functools

import numpy as np
import jax
import jax.numpy as jnp
from jax import lax
from jax.experimental import pallas as pl
from jax.experimental.pallas import tpu as pltpu

F32 = jnp.float32
BF16 = jnp.bfloat16

D = 2048
N_CTX_B, CTX_T = 32, 256
N_LAT_B, LAT_T = 8, 1024
N_CTX = N_CTX_B * CTX_T
N_LAT = N_LAT_B * LAT_T
N_TOK = N_CTX + N_LAT
DEPTH = 2
PAST = 512
GRID_W = 64
GLA_H, GLA_DK, GLA_DV, GLA_RANK, GLA_TAU, GLA_C = 4, 64, 128, 16, 16.0, 32
FFT_G, FFT_CH = 4, 128
N_HEADS, N_KV, HEAD_DIM = 8, 2, 128
GQA = N_HEADS // N_KV
WINDOW, ATT_BLOCK = 128, 128
ROPE_BASE = 10000.0
D_FF = 5632
N_EXP = 8
EPS = 1e-6
NEG_INF = -1e30

Z_GQ, Z_GK, Z_GV, Z_GG, Z_FF, Z_AQ, Z_AK, Z_AV, Z_LR = 0, 256, 512, 1024, 1536, 2048, 3072, 3328, 3584
Z_W = 3840
LANE = 128

VMEM_LIMIT = 56 * 1024 * 1024

MOE_TM = 1024
MOE_TILES = 2 * N_TOK // MOE_TM + N_EXP
MOE_ROWS = MOE_TILES * MOE_TM


def _cparams(sem):
    return pltpu.CompilerParams(dimension_semantics=sem, vmem_limit_bytes=VMEM_LIMIT)


def _silu(x):
    return x / (1.0 + jnp.exp(-x))


def _norm_mod(x, nw, scale, shift):
    ms = jnp.mean(x * x, axis=-1, keepdims=True)
    return (x * lax.rsqrt(ms + EPS) * nw) * (1.0 + scale) + shift


def _dot(a, b):
    return jnp.dot(a, b, preferred_element_type=F32)


def _dot_nt(a, b):
    return lax.dot_general(a, b, (((1,), (1,)), ((), ())), preferred_element_type=F32)


def _dot_tn(a, b):
    return lax.dot_general(a, b, (((0,), (0,)), ((), ())), preferred_element_type=F32)


def _mod_body(c_ref, w_ref, b_ref, o_ref):
    s = _silu(c_ref[...])
    o_ref[...] = _dot(s.astype(BF16), w_ref[...].astype(BF16)) + b_ref[...]


def _mods(cond16, w_ada, b_ada):
    tn = 1024
    return pl.pallas_call(
        _mod_body,
        grid=(DEPTH, 6 * D // tn),
        in_specs=[pl.BlockSpec((16, D), lambda l, j: (0, 0)),
                  pl.BlockSpec((None, D, tn), lambda l, j: (l, 0, j)),
                  pl.BlockSpec((None, 1, tn), lambda l, j: (l, 0, j))],
        out_specs=pl.BlockSpec((None, 16, tn), lambda l, j: (l, 0, j)),
        out_shape=jax.ShapeDtypeStruct((DEPTH, 16, 6 * D), F32),
        compiler_params=_cparams(("arbitrary", "arbitrary")),
        name="adaln_mod",
    )(cond16, w_ada, b_ada.reshape(DEPTH, 1, 6 * D))


def _mod_row(i, tm):
    per_lat = LAT_T // tm
    n_ctx_tiles = N_CTX // tm
    return jnp.where(i < n_ctx_tiles, 0, 1 + (i - n_ctx_tiles) // per_lat)


def _split_x(x, tm):
    if isinstance(x, (tuple, list)):
        return x[0], x[1], 0
    return x, x, N_CTX // tm


def _x_specs(tm, off_b, width):
    n_ctx_tiles = N_CTX // tm
    return [pl.BlockSpec((tm, width), lambda i: (jnp.minimum(i, n_ctx_tiles - 1), 0)),
            pl.BlockSpec((tm, width), lambda i: (jnp.maximum(i - n_ctx_tiles, 0) + off_b, 0))]


def _pick_x(xa_ref, xb_ref, tm):
    return jnp.where(pl.program_id(0) < N_CTX // tm, xa_ref[...], xb_ref[...])


def _inproj_body(xa_ref, xb_ref, mod_ref, n_ref, w_ref, o_ref, *, tm):
    h = _norm_mod(_pick_x(xa_ref, xb_ref, tm), n_ref[...], mod_ref[1:2, :], mod_ref[0:1, :])
    o_ref[...] = _dot(h.astype(BF16), w_ref[...]).astype(BF16)


def _inproj(x, mod, n1, w_in_b):
    tm = 512
    xa, xb, off_b = _split_x(x, tm)
    return pl.pallas_call(
        functools.partial(_inproj_body, tm=tm),
        grid=(N_TOK // tm,),
        in_specs=_x_specs(tm, off_b, D) + [
            pl.BlockSpec((None, 6, D), lambda i: (_mod_row(i, tm), 0, 0)),
            pl.BlockSpec((1, D), lambda i: (0, 0)),
            pl.BlockSpec((D, Z_W), lambda i: (0, 0), pipeline_mode=pl.Buffered(1))],
        out_specs=pl.BlockSpec((tm, Z_W), lambda i: (i, 0)),
        out_shape=jax.ShapeDtypeStruct((N_TOK, Z_W), BF16),
        compiler_params=_cparams(("arbitrary",)),
        name="inproj",
    )(xa, xb, mod, n1.reshape(1, D), w_in_b)


def _gla_body(qk_ref, v_ref, gg_ref, lr_ref, wlr_ref, blr_ref, s0_ref, nw_ref, y_ref, st_ref,
              g_s, o_s, st_s, qt_s, ke_s, dec_s, *, T, U):
    C = GLA_C
    NC = T // C
    RT = U * T
    qw = GLA_H * GLA_DK
    lr = lr_ref[...]
    for d in range(2):
        x = _dot(lr, wlr_ref[d]) + blr_ref[d]
        g_s[d] = (jnp.minimum(x, 0.0) - jnp.log1p(jnp.exp(-jnp.abs(x)))) * (1.0 / GLA_TAU)
    st_s[...] = s0_ref[...]

    R = 128
    cpb = R // C
    ri = lax.broadcasted_iota(jnp.int32, (R, R), 0)
    ci = lax.broadcasted_iota(jnp.int32, (R, R), 1)
    same = (ri // C) == (ci // C)
    masks = (jnp.logical_and(same, ci <= ri), jnp.logical_and(same, ci >= ri))
    sum_ops = tuple(jnp.concatenate([m.astype(BF16), same.astype(BF16)], axis=0) for m in masks)

    def intra(i, carry):
        r0 = pl.multiple_of(i * R, R)
        q = qk_ref[pl.ds(r0, R), 0:qw].astype(F32) * (GLA_DK ** -0.5)
        k = qk_ref[pl.ds(r0, R), qw:2 * qw].astype(F32)
        vb = v_ref[pl.ds(r0, R), :]
        heads = [(d, h) for d in range(2) for h in range(GLA_H)]
        ks = [slice(h * GLA_DK, (h + 1) * GLA_DK) for h in range(GLA_H)]
        sums = []
        for d in range(2):
            g = g_s[d, pl.ds(r0, R), :]
            g_hi = g.astype(BF16)
            r1 = g - g_hi.astype(F32)
            g_mid = r1.astype(BF16)
            g_lo = (r1 - g_mid.astype(F32)).astype(BF16)
            sums.append(_dot(sum_ops[d], jnp.concatenate([g_hi, g_mid, g_lo], axis=1)))
        qts, kts = [], []
        for d in range(2):
            s3 = sums[d][:, 0:qw] + sums[d][:, qw:2 * qw] + sums[d][:, 2 * qw:3 * qw]
            b = s3[0:R]
            bl = s3[R:2 * R]
            qts.append((q * jnp.exp(b)).astype(BF16))
            kts.append((k * jnp.exp(-b)).astype(BF16))
            qt_s[d, pl.ds(r0, R), :] = qts[d]
            ke_s[d, pl.ds(r0, R), :] = (k * jnp.exp(bl - b)).astype(BF16)
            ebl = jnp.exp(bl)
            for j in range(cpb):
                dec_s[d, pl.ds(i * cpb + j, 1), :] = ebl[j * C:j * C + 1, :]
        a = [_dot_nt(qts[d][:, ks[h]], kts[d][:, ks[h]]) for d, h in heads]
        a = [jnp.where(masks[d], a[n], 0.0).astype(BF16) for n, (d, h) in enumerate(heads)]
        o = [_dot(a[n], vb[:, h * GLA_DV:(h + 1) * GLA_DV]) for n, (d, h) in enumerate(heads)]
        for d in range(2):
            o_s[d, pl.ds(r0, R), :] = jnp.concatenate(o[d * GLA_H:(d + 1) * GLA_H], axis=1)
        return carry

    lax.fori_loop(0, RT // R, intra, 0)

    def inter(i, carry):
        ks = [slice(h * GLA_DK, (h + 1) * GLA_DK) for h in range(GLA_H)]
        for u in range(U):
            for d in range(2):
                c = u * NC + (i if d == 0 else NC - 1 - i)
                r0 = pl.multiple_of(c * C, C)
                qt = qt_s[d, pl.ds(r0, C), :]
                ke = ke_s[d, pl.ds(r0, C), :]
                vb = v_ref[pl.ds(r0, C), :]
                dec = dec_s[d, pl.ds(c, 1), :]
                o = []
                for h in range(GLA_H):
                    s_t = st_s[u, d, h]
                    o.append(_dot_nt(qt[:, ks[h]], s_t.astype(BF16)))
                    u_t = _dot_tn(vb[:, h * GLA_DV:(h + 1) * GLA_DV], ke[:, ks[h]])
                    st_s[u, d, h] = s_t * dec[:, ks[h]] + u_t
                o_s[d, pl.ds(r0, C), :] += jnp.concatenate(o, axis=1)
        return carry

    lax.fori_loop(0, NC, inter, 0)

    ch = 128
    nw = nw_ref[...]

    def epi(i, carry):
        r = pl.multiple_of(i * ch, ch)
        for h in range(GLA_H):
            cs = slice(h * GLA_DV, (h + 1) * GLA_DV)
            o = o_s[0, pl.ds(r, ch), cs] + o_s[1, pl.ds(r, ch), cs]
            ms = jnp.mean(o * o, axis=-1, keepdims=True)
            y = (o * lax.rsqrt(ms + EPS) * nw) * _silu(gg_ref[pl.ds(r, ch), cs].astype(F32))
            y_ref[pl.ds(r, ch), cs] = y.astype(BF16)
        return carry

    lax.fori_loop(0, RT // ch, epi, 0)
    st_ref[...] = st_s[...]


GLA_UNITS = {CTX_T: 4, LAT_T: 2}


def _gla(z, wlr, blr, s0_t, nw, *, T, n_units, unit0):
    U = GLA_UNITS[T]
    RT = U * T
    rb = lambda s: s + unit0 * T // RT
    hv = GLA_H * GLA_DV
    qw = GLA_H * GLA_DK
    st_spec = pl.BlockSpec((U, 2, GLA_H, GLA_DV, GLA_DK), lambda s: (s, 0, 0, 0, 0))
    in_specs = [pl.BlockSpec((RT, 2 * qw), lambda s: (rb(s), Z_GQ // (2 * qw))),
                pl.BlockSpec((RT, hv), lambda s: (rb(s), Z_GV // hv)),
                pl.BlockSpec((RT, hv), lambda s: (rb(s), Z_GG // hv)),
                pl.BlockSpec((RT, LANE), lambda s: (rb(s), Z_LR // LANE)),
                pl.BlockSpec((2, LANE, qw), lambda s: (0, 0, 0)),
                pl.BlockSpec((2, 1, qw), lambda s: (0, 0, 0)),
                st_spec,
                pl.BlockSpec((1, GLA_DV), lambda s: (0, 0))]
    args = [z, z, z, z, wlr, blr, s0_t, nw.reshape(1, GLA_DV)]
    return pl.pallas_call(
        functools.partial(_gla_body, T=T, U=U),
        grid=(n_units // U,),
        in_specs=in_specs,
        out_specs=[pl.BlockSpec((RT, hv), lambda s: (s, 0)), st_spec],
        out_shape=[jax.ShapeDtypeStruct((n_units * T, hv), BF16),
                   jax.ShapeDtypeStruct((n_units, 2, GLA_H, GLA_DV, GLA_DK), F32)],
        scratch_shapes=[pltpu.VMEM((2, RT, qw), F32),
                        pltpu.VMEM((2, RT, hv), F32),
                        pltpu.VMEM((U, 2, GLA_H, GLA_DV, GLA_DK), F32),
                        pltpu.VMEM((2, RT, qw), BF16),
                        pltpu.VMEM((2, RT, qw), BF16),
                        pltpu.VMEM((2, RT // GLA_C, qw), F32)],
        compiler_params=_cparams(("arbitrary",)),
        name=f"gla_T{T}",
    )(*args)


def _fft_body(x_ref, w2_ref, dt_ref, y_ref, p_s, *, T):
    xb = x_ref[...]
    for g in range(FFT_G):
        cs = slice(g * FFT_CH, (g + 1) * FFT_CH)
        p = _dot(xb[:, cs], w2_ref[...])
        p_s[0:T, cs] = p[:, :FFT_CH].astype(BF16)
        p_s[T:2 * T, cs] = p[:, FFT_CH:].astype(BF16)
    y_ref[...] = _dot(dt_ref[...], p_s[...]).astype(BF16)


def _dft_consts(T):
    c = np.arange(FFT_CH)
    ang_c = (np.outer(c, c) % FFT_CH) * (2.0 * np.pi / FFT_CH)
    w2 = np.concatenate([np.cos(ang_c), np.sin(ang_c)], axis=1) / np.sqrt(FFT_CH)
    t = np.arange(T)
    ang_t = (np.outer(t, t) % T) * (2.0 * np.pi / T)
    dt = np.concatenate([np.cos(ang_t), -np.sin(ang_t)], axis=1) / np.sqrt(T)
    return jnp.asarray(w2, F32).astype(BF16), jnp.asarray(dt, F32).astype(BF16)


def _fft(z, *, T, n_units, unit0):
    w2, dt = _dft_consts(T)
    fw = FFT_G * FFT_CH
    in_specs = [pl.BlockSpec((T, fw), lambda u: (u + unit0, Z_FF // fw)),
                pl.BlockSpec((FFT_CH, 2 * FFT_CH), lambda u: (0, 0)),
                pl.BlockSpec((T, 2 * T), lambda u: (0, 0))]
    args = [z, w2, dt]
    return pl.pallas_call(
        functools.partial(_fft_body, T=T),
        grid=(n_units,),
        in_specs=in_specs,
        out_specs=pl.BlockSpec((T, fw), lambda u: (u, 0)),
        out_shape=jax.ShapeDtypeStruct((n_units * T, fw), BF16),
        scratch_shapes=[pltpu.VMEM((2 * T, fw), BF16)],
        compiler_params=_cparams(("arbitrary",)),
        name=f"fft_T{T}",
    )(*args)


def _rope_tables(T):
    half = HEAD_DIM // 2
    inv = ROPE_BASE ** (-np.arange(0, half, 2, dtype=np.float64) / half)
    t = np.arange(T)
    ang_r = (t // GRID_W)[:, None] * inv[None, :]
    ang_c = (t % GRID_W)[:, None] * inv[None, :]
    cos = np.concatenate([np.cos(ang_r)] * 2 + [np.cos(ang_c)] * 2, axis=1)
    sin = np.concatenate([-np.sin(ang_r), np.sin(ang_r), -np.sin(ang_c), np.sin(ang_c)], axis=1)
    return jnp.asarray(cos, F32), jnp.asarray(sin, F32)


def _rope(x, cos, sin):
    lane = lax.broadcasted_iota(jnp.int32, x.shape, 1)
    quarter = HEAD_DIM // 4
    partner = jnp.where(lane % (2 * quarter) < quarter,
                        pltpu.roll(x, HEAD_DIM - quarter, 1), pltpu.roll(x, quarter, 1))
    return x * cos + partner * sin


def _rms128(x, w):
    return x * lax.rsqrt(jnp.mean(x * x, axis=-1, keepdims=True) + EPS) * w


def _lane_fold_max(s):
    m = s[:, 0:LANE]
    for j in range(1, s.shape[1] // LANE):
        m = jnp.maximum(m, s[:, j * LANE:(j + 1) * LANE])
    return m


def _row_max(s):
    return jnp.max(s, axis=-1, keepdims=True)


def _attn_body(*refs, T, latent):
    refs, (s_s, p_s, em_s, bias_s) = refs[:-4], refs[-4:]
    if latent:
        (sink_ref, q_ref, k_ref, v_ref, qw_ref, kw_ref, ck_ref, cv_ref, cos_ref, sin_ref, y_ref) = refs
    else:
        (sink_ref, q_ref, k_ref, v_ref, qw_ref, kw_ref, y_ref, ko_ref, vo_ref) = refs
    kv = pl.program_id(1)
    nb = T // ATT_BLOCK
    kn = _rms128(k_ref[...].astype(F32), kw_ref[...])
    vb = v_ref[...]
    nq = GQA * ATT_BLOCK
    v1 = jnp.concatenate([vb, jnp.ones_like(vb)], axis=1)
    if latent:
        kn = _rope(kn, cos_ref[...], sin_ref[...])
        ck_t = ck_ref[...].T.astype(BF16)
        cvb = cv_ref[...].astype(BF16)
        cv1 = jnp.concatenate([cvb, jnp.ones_like(cvb)], axis=1)
        assert WINDOW == ATT_BLOCK
        q_off = lax.broadcasted_iota(jnp.int32, (nq, ATT_BLOCK), 0) % ATT_BLOCK
        k_off = lax.broadcasted_iota(jnp.int32, (nq, ATT_BLOCK), 1)
        band = jnp.concatenate([k_off >= q_off, k_off >= 0, k_off <= q_off], axis=1)
        bias_s[...] = jnp.where(band, 0.0, NEG_INF)
    else:
        ko_ref[...] = kn
        vo_ref[...] = vb.astype(F32)
    k_t = kn.T.astype(BF16)
    scale = HEAD_DIM ** -0.5
    rc = 32

    def key_range(n):
        lo, hi = (max(n - 1, 0), min(n + 2, nb)) if latent else (0, nb)
        return slice(lo * ATT_BLOCK, hi * ATT_BLOCK), slice((lo - n + 1) * ATT_BLOCK, (hi - n + 1) * ATT_BLOCK)

    def scores(n):
        rs = slice(n * ATT_BLOCK, (n + 1) * ATT_BLOCK)
        qs = []
        for g in range(GQA):
            qn = _rms128(q_ref[rs, g * HEAD_DIM:(g + 1) * HEAD_DIM].astype(F32), qw_ref[...])
            if latent:
                qn = _rope(qn, cos_ref[rs, :], sin_ref[rs, :])
            qs.append((qn * scale).astype(BF16))
        qq = jnp.concatenate(qs, axis=0)
        ks, _ = key_range(n)
        w = ks.stop - ks.start
        s_s[n % 2, :, 0:w] = _dot(qq, k_t[:, ks])
        if latent:
            s_s[n % 2, :, w:w + PAST] = _dot(qq, ck_t)

    def finish(n):
        rs = slice(n * ATT_BLOCK, (n + 1) * ATT_BLOCK)
        ks, ms = key_range(n)
        w = ks.stop - ks.start
        wt = w + (PAST if latent else 0)

        for c in range(nq // rc):
            cr = slice(c * rc, (c + 1) * rc)
            s = s_s[n % 2, cr, 0:wt]
            if latent:
                s = jnp.concatenate([s[:, 0:w] + bias_s[cr, ms], s[:, w:]], axis=1)
            sink = sink_ref[kv * GQA + c // (ATT_BLOCK // rc)]
            m = jnp.maximum(_row_max(_lane_fold_max(s)), sink)
            p_s[cr, 0:wt] = jnp.exp(s - m).astype(BF16)
            em_s[cr, :] = jnp.broadcast_to(jnp.exp(sink - m), (rc, LANE))
        ov = _dot(p_s[:, 0:w], v1[ks])
        if latent:
            ov = ov + _dot(p_s[:, w:wt], cv1)
        den = ov[:, HEAD_DIM:HEAD_DIM + 1] + em_s[:, 0:1]
        o = ov[:, :HEAD_DIM] * (1.0 / den)
        for g in range(GQA):
            y_ref[rs, g * HEAD_DIM:(g + 1) * HEAD_DIM] = o[g * ATT_BLOCK:(g + 1) * ATT_BLOCK].astype(BF16)

    scores(0)
    for n in range(nb):
        if n + 1 < nb:
            scores(n + 1)
        finish(n)


def _attn(z, sink_l, qw, kw, *, T, n_units, unit0, ctx_k=None, ctx_v=None):
    latent = ctx_k is not None
    qwid = GQA * HEAD_DIM
    rb = lambda u: u + unit0
    smem = pl.BlockSpec(memory_space=pltpu.SMEM)
    in_specs = [smem,
                pl.BlockSpec((T, qwid), lambda u, h: (rb(u), Z_AQ // qwid + h)),
                pl.BlockSpec((T, HEAD_DIM), lambda u, h: (rb(u), Z_AK // HEAD_DIM + h)),
                pl.BlockSpec((T, HEAD_DIM), lambda u, h: (rb(u), Z_AV // HEAD_DIM + h)),
                pl.BlockSpec((1, HEAD_DIM), lambda u, h: (0, 0)),
                pl.BlockSpec((1, HEAD_DIM), lambda u, h: (0, 0))]
    args = [sink_l, z, z, z, qw.reshape(1, HEAD_DIM), kw.reshape(1, HEAD_DIM)]
    y_spec = pl.BlockSpec((T, qwid), lambda u, h: (u, h))
    y_shape = jax.ShapeDtypeStruct((n_units * T, N_HEADS * HEAD_DIM), BF16)
    if latent:
        cos, sin = _rope_tables(T)
        in_specs += [pl.BlockSpec((None, PAST, HEAD_DIM), lambda u, h: (u, 0, h)),
                     pl.BlockSpec((None, PAST, HEAD_DIM), lambda u, h: (u, 0, h)),
                     pl.BlockSpec((T, HEAD_DIM), lambda u, h: (0, 0)),
                     pl.BlockSpec((T, HEAD_DIM), lambda u, h: (0, 0))]
        args += [ctx_k, ctx_v, cos, sin]
        out_specs, out_shape = y_spec, y_shape
    else:
        kv_spec = pl.BlockSpec((T, HEAD_DIM), lambda u, h: (u, h))
        kv_shape = jax.ShapeDtypeStruct((n_units * T, N_KV * HEAD_DIM), F32)
        out_specs, out_shape = [y_spec, kv_spec, kv_spec], [y_shape, kv_shape, kv_shape]
    nq = GQA * ATT_BLOCK
    key_w = 3 * ATT_BLOCK + PAST if latent else T
    scratch = [pltpu.VMEM((2, nq, key_w), F32),
               pltpu.VMEM((nq, key_w), BF16),
               pltpu.VMEM((nq, LANE), F32),
               pltpu.VMEM((nq, 3 * ATT_BLOCK), F32)]
    return pl.pallas_call(
        functools.partial(_attn_body, T=T, latent=latent),
        grid=(n_units, N_KV),
        in_specs=in_specs,
        out_specs=out_specs,
        out_shape=out_shape,
        scratch_shapes=scratch,
        compiler_params=_cparams(("arbitrary", "arbitrary")),
        name=f"attn_T{T}",
    )(*args)


def _outproj_body(*refs, router, tm):
    y_refs, refs = refs[:6], refs[6:]
    if router:
        (w_ref, xa_ref, xb_ref, mod_ref, n2_ref, r_ref, x1_ref, h2_ref, idx_ref, gate_ref, cnt_ref, cnt_s) = refs
    else:
        (w_ref, xa_ref, xb_ref, mod_ref, n2_ref, x1_ref, h2_ref) = refs
    gw = GLA_H * GLA_DV
    fw = FFT_G * FFT_CH
    yg, yf, ya = (_pick_x(y_refs[2 * j], y_refs[2 * j + 1], tm) for j in range(3))
    acc = _dot(yg, w_ref[0:gw, :]) + _dot(yf, w_ref[gw:gw + fw, :]) + _dot(ya, w_ref[gw + fw:, :])
    x1 = _pick_x(xa_ref, xb_ref, tm) + mod_ref[2:3, :] * acc
    x1_ref[...] = x1
    h2 = _norm_mod(x1, n2_ref[...], mod_ref[4:5, :], mod_ref[3:4, :])
    h2_ref[...] = h2.astype(h2_ref.dtype)
    if router:
        logits = _dot(h2.astype(BF16), r_ref[...])
        lane = lax.broadcasted_iota(jnp.int32, logits.shape, 1)
        lg = jnp.where(lane < N_EXP, logits, -jnp.inf)
        v1 = jnp.max(lg, axis=-1, keepdims=True)
        i1 = jnp.min(jnp.where(lg == v1, lane, LANE), axis=-1, keepdims=True)
        lg2 = jnp.where(lane == i1, -jnp.inf, lg)
        v2 = jnp.max(lg2, axis=-1, keepdims=True)
        i2 = jnp.min(jnp.where(lg2 == v2, lane, LANE), axis=-1, keepdims=True)
        e = jnp.exp(v2 - v1)
        g1 = 1.0 / (1.0 + e)
        gate_ref[...] = jnp.where(lane == 0, g1, jnp.where(lane == 1, e * g1, 0.0))

        @pl.when(pl.program_id(0) == 0)
        def _():
            cnt_s[...] = jnp.zeros_like(cnt_s)

        oh1 = lane == i1
        oh2 = lane == i2
        oh = jnp.where(jnp.logical_or(oh1, oh2), 1.0, 0.0)
        ri = lax.broadcasted_iota(jnp.int32, (tm, tm), 0)
        ci = lax.broadcasted_iota(jnp.int32, (tm, tm), 1)
        before = jnp.where(ci < ri, 1.0, 0.0).astype(BF16)
        prior = _dot(before, oh.astype(BF16)) + cnt_s[0:1, :]
        r1 = jnp.sum(jnp.where(oh1, prior, 0.0), axis=-1, keepdims=True).astype(jnp.int32)
        r2 = jnp.sum(jnp.where(oh2, prior, 0.0), axis=-1, keepdims=True).astype(jnp.int32)
        idx_ref[...] = jnp.where(lane == 0, i1, jnp.where(lane == 1, i2,
                                 jnp.where(lane == 2, r1, jnp.where(lane == 3, r2, 0))))
        cnt_s[...] = cnt_s[...] + jnp.sum(oh, axis=0, keepdims=True)
        cnt_ref[...] = cnt_s[...]


def _outproj(y_gla, y_fft, y_att, w_out_b, x, mod, n2, router_b=None):
    tm = 512
    router = router_b is not None
    gw, fw, aw = GLA_H * GLA_DV, FFT_G * FFT_CH, N_HEADS * HEAD_DIM
    row = lambda i: (i, 0)
    xa, xb, off_b = _split_x(x, tm)
    in_specs, args = [], []
    for pair, wid in ((y_gla, gw), (y_fft, fw), (y_att, aw)):
        in_specs += _x_specs(tm, 0, wid)
        args += list(pair)
    in_specs.append(pl.BlockSpec((gw + fw + aw, D), lambda i: (0, 0), pipeline_mode=pl.Buffered(1)))
    in_specs += _x_specs(tm, off_b, D)
    in_specs += [pl.BlockSpec((None, 6, D), lambda i: (_mod_row(i, tm), 0, 0)),
                 pl.BlockSpec((1, D), lambda i: (0, 0))]
    args += [w_out_b, xa, xb, mod, n2.reshape(1, D)]
    out_specs = [pl.BlockSpec((tm, D), row), pl.BlockSpec((tm, D), row)]
    out_shape = [jax.ShapeDtypeStruct((N_TOK, D), F32),
                 jax.ShapeDtypeStruct((N_TOK, D), F32 if router else BF16)]
    if router:
        in_specs.append(pl.BlockSpec((D, LANE), lambda i: (0, 0)))
        args.append(router_b)
        out_specs += [pl.BlockSpec((tm, LANE), row), pl.BlockSpec((tm, LANE), row),
                      pl.BlockSpec((8, LANE), lambda i: (0, 0))]
        out_shape += [jax.ShapeDtypeStruct((N_TOK, LANE), jnp.int32), jax.ShapeDtypeStruct((N_TOK, LANE), F32),
                      jax.ShapeDtypeStruct((8, LANE), F32)]
    return pl.pallas_call(
        functools.partial(_outproj_body, router=router, tm=tm),
        grid=(N_TOK // tm,),
        in_specs=in_specs,
        out_specs=out_specs,
        out_shape=out_shape,
        scratch_shapes=[pltpu.VMEM((8, LANE), F32)] if router else [],
        compiler_params=_cparams(("arbitrary",)),
        name="outproj_router" if router else "outproj",
    )(*args)


FF_TF = 512
FF_KA = D_FF // FF_TF


def _store_f_slice(g_s, g, s, rows=slice(None)):
    for kk in range(FF_KA):
        @pl.when(s == kk)
        def _(kk=kk):
            g_s[rows, kk * FF_TF:(kk + 1) * FF_TF] = g


def _ffn_body(h_ref, w1_ref, w3_ref, w2_ref, x_ref, mod_ref, o_ref, g_s):
    s = pl.program_id(1)

    @pl.when(s < FF_KA)
    def _():
        h = h_ref[...]
        g = (_silu(_dot(h, w1_ref[...])) * _dot(h, w3_ref[...])).astype(BF16)
        _store_f_slice(g_s, g, s)

    @pl.when(s >= FF_KA)
    def _():
        o_ref[...] = x_ref[...] + mod_ref[5:6, :] * _dot(g_s[...], w2_ref[...])


def _ffn(h2, w1_b, w3_b, w2_b, x1, mod):
    tm, tn = 1024, 512
    kb = D // tn
    up = lambda i, s: (0, jnp.minimum(s, FF_KA - 1))
    down = lambda s: jnp.maximum(s - FF_KA, 0)
    return pl.pallas_call(
        _ffn_body,
        grid=(N_TOK // tm, FF_KA + kb),
        in_specs=[pl.BlockSpec((tm, D), lambda i, s: (i, 0)),
                  pl.BlockSpec((D, FF_TF), up),
                  pl.BlockSpec((D, FF_TF), up),
                  pl.BlockSpec((D_FF, tn), lambda i, s: (0, down(s))),
                  pl.BlockSpec((tm, tn), lambda i, s: (i, down(s))),
                  pl.BlockSpec((None, 6, tn), lambda i, s: (_mod_row(i, tm), 0, down(s)))],
        out_specs=pl.BlockSpec((tm, tn), lambda i, s: (i, down(s))),
        out_shape=jax.ShapeDtypeStruct((N_TOK, D), F32),
        scratch_shapes=[pltpu.VMEM((tm, D_FF), BF16)],
        compiler_params=_cparams(("arbitrary", "arbitrary")),
        name="ffn_dense",
    )(h2, w1_b, w3_b, w2_b, x1, mod)


MOE_TN = 256
MOE_KB = D // MOE_TN
MOE_GATHER = 512


def _moe_body(te_ref, tr_ref, src_ref, h_hbm, w1_ref, w3_ref, w2_ref, y_ref, xg_s, xb_s, g_s, sem):
    i = pl.program_id(0)
    s = pl.program_id(1)
    n_rows = tr_ref[i]
    lower = slice(0, MOE_GATHER)
    cases = ((slice(None), n_rows > MOE_GATHER),
             (lower, jnp.logical_and(n_rows > 0, n_rows <= MOE_GATHER)))

    def row_copy(r, t):
        return pltpu.make_async_copy(h_hbm.at[pl.ds(t, 1)], xg_s.at[pl.ds(r, 1)], sem)

    for p in range(MOE_TM // MOE_GATHER):
        @pl.when(jnp.logical_and(n_rows > p * MOE_GATHER, s == 0))
        def _(p=p):
            base = i * MOE_TM + p * MOE_GATHER

            def issue(r, c):
                row_copy(r, src_ref[base + r]).start()
                return c

            lax.fori_loop(0, MOE_GATHER, issue, 0, unroll=8)
            pltpu.make_async_copy(h_hbm.at[pl.ds(0, MOE_GATHER)], xg_s, sem).wait()
            xb_s[p * MOE_GATHER:(p + 1) * MOE_GATHER, :] = xg_s[...].astype(BF16)

    for rows, cond in cases:
        @pl.when(jnp.logical_and(cond, s < FF_KA))
        def _(rows=rows):
            h = xb_s[rows, :]
            g = (_silu(_dot(h, w1_ref[...].astype(BF16))) * _dot(h, w3_ref[...].astype(BF16))).astype(BF16)
            _store_f_slice(g_s, g, s, rows)

        @pl.when(jnp.logical_and(cond, s >= FF_KA))
        def _(rows=rows):
            y_ref[rows, :] = _dot(g_s[rows, :], w2_ref[...].astype(BF16))

    @pl.when(n_rows <= MOE_GATHER)
    def _():
        y_ref[MOE_GATHER:, :] = jnp.zeros((MOE_TM - MOE_GATHER, MOE_TN), F32)

    @pl.when(n_rows == 0)
    def _():
        y_ref[lower, :] = jnp.zeros((MOE_GATHER, MOE_TN), F32)


def _moe(tile_exp, tile_rows, row_src, h2, w1, w3, w2):
    def up(i, s, te, tr, src):
        return (te[i], 0, jnp.where(tr[i] > 0, jnp.minimum(s, FF_KA - 1), FF_KA - 1))

    def down(i, s, te, tr, src):
        return (te[i], 0, jnp.where(tr[i] > 0, jnp.maximum(s - FF_KA, 0), MOE_KB - 1))

    return pl.pallas_call(
        _moe_body,
        grid_spec=pltpu.PrefetchScalarGridSpec(
            num_scalar_prefetch=3,
            grid=(MOE_TILES, FF_KA + MOE_KB),
            in_specs=[pl.BlockSpec(memory_space=pl.ANY),
                      pl.BlockSpec((None, D, FF_TF), up),
                      pl.BlockSpec((None, D, FF_TF), up),
                      pl.BlockSpec((None, D_FF, MOE_TN), down)],
            out_specs=pl.BlockSpec((MOE_TM, MOE_TN), lambda i, s, te, tr, src: (i, jnp.maximum(s - FF_KA, 0))),
            scratch_shapes=[pltpu.VMEM((MOE_GATHER, D), F32), pltpu.VMEM((MOE_TM, D), BF16),
                            pltpu.VMEM((MOE_TM, D_FF), BF16), pltpu.SemaphoreType.DMA(())],
        ),
        out_shape=jax.ShapeDtypeStruct((MOE_ROWS, D), F32),
        compiler_params=_cparams(("arbitrary", "arbitrary")),
        name="moe_experts",
    )(tile_exp, tile_rows, row_src, h2, w1, w3, w2)


def _combine_body(slot_ref, y_hbm, x_ref, g_ref, mod_ref, oc_ref, ol_ref, buf, sems, *, tc):
    i = pl.program_id(0)

    def gather(tile, b):
        base = tile * tc

        def issue(r, c):
            for k in range(2):
                pltpu.make_async_copy(y_hbm.at[pl.ds(slot_ref[2 * (base + r) + k], 1)],
                                      buf.at[b, k, pl.ds(r, 1)], sems.at[b]).start()
            return c

        lax.fori_loop(0, tc, issue, 0, unroll=8)

    def wait(b):
        for k in range(2):
            pltpu.make_async_copy(y_hbm.at[pl.ds(0, tc)], buf.at[b, k], sems.at[b]).wait()

    @pl.when(i == 0)
    def _():
        gather(0, 0)

    for b in range(2):
        @pl.when(jnp.logical_and(i % 2 == b, i + 1 < pl.num_programs(0)))
        def _(b=b):
            gather(i + 1, 1 - b)

    for b in range(2):
        @pl.when(i % 2 == b)
        def _(b=b):
            wait(b)
            g = g_ref[...]
            f = g[:, 0:1] * buf[b, 0] + g[:, 1:2] * buf[b, 1]
            _combine_store(x_ref[...] + mod_ref[5:6, :] * f, i, oc_ref, ol_ref, tc)


def _combine_store(out, i, oc_ref, ol_ref, tc):
    is_ctx = i < N_CTX // tc

    @pl.when(is_ctx)
    def _():
        oc_ref[...] = out

    @pl.when(jnp.logical_not(is_ctx))
    def _():
        ol_ref[...] = out


def _combine(slot, y, x1, gates, mod):
    tc = 256
    n_ctx_tiles = N_CTX // tc
    return pl.pallas_call(
        functools.partial(_combine_body, tc=tc),
        grid_spec=pltpu.PrefetchScalarGridSpec(
            num_scalar_prefetch=1,
            grid=(N_TOK // tc,),
            in_specs=[pl.BlockSpec(memory_space=pl.ANY),
                      pl.BlockSpec((tc, D), lambda i, s: (i, 0)),
                      pl.BlockSpec((tc, LANE), lambda i, s: (i, 0)),
                      pl.BlockSpec((None, 6, D), lambda i, s: (_mod_row(i, tc), 0, 0))],
            out_specs=[pl.BlockSpec((tc, D), lambda i, s: (jnp.minimum(i, n_ctx_tiles - 1), 0)),
                       pl.BlockSpec((tc, D), lambda i, s: (jnp.maximum(i - n_ctx_tiles, 0), 0))],
            scratch_shapes=[pltpu.VMEM((2, 2, tc, D), F32), pltpu.SemaphoreType.DMA((2,))],
        ),
        out_shape=[jax.ShapeDtypeStruct((N_CTX, D), F32), jax.ShapeDtypeStruct((N_LAT, D), F32)],
        compiler_params=_cparams(("arbitrary",)),
        name="moe_combine",
    )(slot, y, x1, gates, mod)


def _route_meta(idx4, counts):
    counts = counts.astype(jnp.int32)
    tiles = (counts + MOE_TM - 1) // MOE_TM
    tile_end = jnp.cumsum(tiles)
    tile_start = tile_end - tiles
    n_tiles = tile_end[-1]
    exp_of = idx4[:, 0:2].reshape(-1)
    rank = idx4[:, 2:4].reshape(-1)
    first_row = jnp.sum(jnp.where(exp_of[:, None] == jnp.arange(N_EXP, dtype=jnp.int32)[None, :],
                                  (tile_start * MOE_TM)[None, :], 0), axis=1)
    slot = (first_row + rank).astype(jnp.int32)
    j = jnp.arange(MOE_TILES, dtype=jnp.int32)
    te = jnp.sum((j[:, None] >= tile_end[None, :]).astype(jnp.int32), axis=1)
    te = jnp.where(j < n_tiles, te, te[jnp.maximum(n_tiles - 1, 0)])
    te = jnp.minimum(te, N_EXP - 1).astype(jnp.int32)
    rows = jnp.clip(counts[te] - (j - tile_start[te]) * MOE_TM, 0, MOE_TM)
    rows = jnp.where(j < n_tiles, rows, 0).astype(jnp.int32)
    tok = jnp.arange(2 * N_TOK, dtype=jnp.int32) // 2
    row_src = jnp.zeros((MOE_ROWS,), jnp.int32).at[slot].set(tok, unique_indices=True)
    return te, rows, row_src, slot


def _prep_w_in(w):
    s = np.cumsum([0, 256, 256, 512, 512, 32, 512, 1024, 256, 256])
    gq, gk, gv, gg, lr, ff, aq, ak, av = [w[:, s[i]:s[i + 1]] for i in range(9)]
    pad = jnp.zeros((D, Z_W - Z_LR - 2 * GLA_RANK), w.dtype)
    return jnp.concatenate([gq, gk, gv, gg, ff, aq, ak, av, lr, pad], axis=1).astype(BF16)


def _prep_w_lr(w_lr2):
    out = jnp.zeros((2, LANE, GLA_H * GLA_DK), F32)
    out = out.at[0, 0:GLA_RANK].set(w_lr2[0]).at[1, GLA_RANK:2 * GLA_RANK].set(w_lr2[1])
    return out.astype(BF16)


def kernel(x_prompt, x_sample, cache_k, cache_v, state_gla, c, c_ctx, w_ada, b_ada, norm1_w, norm2_w, w_in, w_gla_lr2, b_gla_lr2, gla_norm_w, q_norm_w, k_norm_w, attn_sink, w_out, ffn_w1, ffn_w3, ffn_w2, moe_router, moe_w1, moe_w3, moe_w2):
    x = (x_prompt.reshape(N_CTX, D), x_sample.reshape(N_LAT, D))
    cond16 = jnp.concatenate([c_ctx[None, :], c, jnp.zeros((16 - 1 - N_LAT_B, D), F32)], axis=0)
    mods = _mods(cond16, w_ada, b_ada).reshape(DEPTH, 16, 6, D)
    ck_all = cache_k.reshape(N_LAT_B, DEPTH, PAST, N_KV * HEAD_DIM)
    cv_all = cache_v.reshape(N_LAT_B, DEPTH, PAST, N_KV * HEAD_DIM)
    s0_all = jnp.swapaxes(state_gla, -1, -2)
    zero_state = jnp.zeros((N_CTX_B, 2, GLA_H, GLA_DV, GLA_DK), F32)
    ctx_units = dict(T=CTX_T, n_units=N_CTX_B, unit0=0)
    lat_units = dict(T=LAT_T, n_units=N_LAT_B, unit0=N_CTX // LAT_T)

    ks, vs, ss = [], [], []
    for l in range(DEPTH):
        mod = mods[l]
        z = _inproj(x, mod, norm1_w[l], _prep_w_in(w_in[l]))
        wlr = _prep_w_lr(w_gla_lr2[l])
        blr = b_gla_lr2[l].reshape(2, 1, GLA_H * GLA_DK)
        yg_c, st_c = _gla(z, wlr, blr, zero_state, gla_norm_w[l], **ctx_units)
        yg_l, _ = _gla(z, wlr, blr, s0_all[:, l], gla_norm_w[l], **lat_units)
        y_gla = (yg_c, yg_l)
        y_fft = (_fft(z, **ctx_units), _fft(z, **lat_units))
        ya_c, k_c, v_c = _attn(z, attn_sink[l], q_norm_w[l], k_norm_w[l], **ctx_units)
        ya_l = _attn(z, attn_sink[l], q_norm_w[l], k_norm_w[l], ctx_k=ck_all[:, l], ctx_v=cv_all[:, l],
                     **lat_units)
        y_att = (ya_c, ya_l)
        w_out_b = w_out[l].astype(BF16)
        e = l // 2
        if l % 2 == 0:
            x1, h2 = _outproj(y_gla, y_fft, y_att, w_out_b, x, mod, norm2_w[l])
            x = _ffn(h2, ffn_w1[e].astype(BF16), ffn_w3[e].astype(BF16), ffn_w2[e].astype(BF16), x1, mod)
        else:
            router_b = jnp.pad(moe_router[e], ((0, 0), (0, LANE - N_EXP))).astype(BF16)
            x1, h2, idx, gates, counts = _outproj(y_gla, y_fft, y_att, w_out_b, x, mod, norm2_w[l], router_b)
            te, tile_rows, row_src, slot = _route_meta(idx[:, :4], counts[0, :N_EXP])
            y = _moe(te, tile_rows, row_src, h2, moe_w1[e], moe_w3[e], moe_w2[e])
            x = _combine(slot, y, x1, gates, mod)
        ks.append(k_c.reshape(N_CTX_B, CTX_T, N_KV, HEAD_DIM))
        vs.append(v_c.reshape(N_CTX_B, CTX_T, N_KV, HEAD_DIM))
        ss.append(jnp.swapaxes(st_c, -1, -2))
    if not isinstance(x, (tuple, list)):
        x = (x[:N_CTX], x[N_CTX:])
    y_prompt = x[0].reshape(N_CTX_B, CTX_T, D)
    y_sample = x[1].reshape(N_LAT_B, LAT_T, D)
    return (y_prompt, y_sample, jnp.stack(ks, axis=1), jnp.stack(vs, axis=1), jnp.stack(ss, axis=1))
```

```python
import functools

import numpy as np
import jax
import jax.numpy as jnp
from jax import lax
from jax.experimental import pallas as pl
from jax.experimental.pallas import tpu as pltpu

F32 = jnp.float32
BF16 = jnp.bfloat16

D = 2048
N_CTX_B, CTX_T = 32, 256
N_LAT_B, LAT_T = 8, 1024
N_CTX = N_CTX_B * CTX_T
N_LAT = N_LAT_B * LAT_T
N_TOK = N_CTX + N_LAT
DEPTH = 2
PAST = 512
GRID_W = 64
GLA_H, GLA_DK, GLA_DV, GLA_RANK, GLA_TAU, GLA_C = 4, 64, 128, 16, 16.0, 32
FFT_G, FFT_CH = 4, 128
N_HEADS, N_KV, HEAD_DIM = 8, 2, 128
GQA = N_HEADS // N_KV
WINDOW, ATT_BLOCK = 128, 128
ROPE_BASE = 10000.0
D_FF = 5632
N_EXP = 8
EPS = 1e-6
NEG_INF = -1e30

Z_GQ, Z_GK, Z_GV, Z_GG, Z_FF, Z_AQ, Z_AK, Z_AV, Z_LR = 0, 256, 512, 1024, 1536, 2048, 3072, 3328, 3584
Z_W = 3840
LANE = 128

VMEM_LIMIT = 56 * 1024 * 1024

MOE_TM = 1024
MOE_TILES = 2 * N_TOK // MOE_TM + N_EXP
MOE_ROWS = MOE_TILES * MOE_TM


def _cparams(sem):
    return pltpu.CompilerParams(dimension_semantics=sem, vmem_limit_bytes=VMEM_LIMIT)


def _silu(x):
    return x / (1.0 + jnp.exp(-x))


def _norm_mod(x, nw, scale, shift):
    ms = jnp.mean(x * x, axis=-1, keepdims=True)
    return (x * lax.rsqrt(ms + EPS) * nw) * (1.0 + scale) + shift


def _dot(a, b):
    return jnp.dot(a, b, preferred_element_type=F32)


def _dot_nt(a, b):
    return lax.dot_general(a, b, (((1,), (1,)), ((), ())), preferred_element_type=F32)


def _dot_tn(a, b):
    return lax.dot_general(a, b, (((0,), (0,)), ((), ())), preferred_element_type=F32)


def _mod_body(c_ref, w_ref, b_ref, o_ref):
    s = _silu(c_ref[...])
    o_ref[...] = _dot(s.astype(BF16), w_ref[...].astype(BF16)) + b_ref[...]


def _mods(cond16, w_ada, b_ada):
    tn = 1024
    return pl.pallas_call(
        _mod_body,
        grid=(DEPTH, 6 * D // tn),
        in_specs=[pl.BlockSpec((16, D), lambda l, j: (0, 0)),
                  pl.BlockSpec((None, D, tn), lambda l, j: (l, 0, j)),
                  pl.BlockSpec((None, 1, tn), lambda l, j: (l, 0, j))],
        out_specs=pl.BlockSpec((None, 16, tn), lambda l, j: (l, 0, j)),
        out_shape=jax.ShapeDtypeStruct((DEPTH, 16, 6 * D), F32),
        compiler_params=_cparams(("arbitrary", "arbitrary")),
        name="adaln_mod",
    )(cond16, w_ada, b_ada.reshape(DEPTH, 1, 6 * D))


def _mod_row(i, tm):
    per_lat = LAT_T // tm
    n_ctx_tiles = N_CTX // tm
    return jnp.where(i < n_ctx_tiles, 0, 1 + (i - n_ctx_tiles) // per_lat)


def _split_x(x, tm):
    if isinstance(x, (tuple, list)):
        return x[0], x[1], 0
    return x, x, N_CTX // tm


def _x_specs(tm, off_b, width):
    n_ctx_tiles = N_CTX // tm
    return [pl.BlockSpec((tm, width), lambda i: (jnp.minimum(i, n_ctx_tiles - 1), 0)),
            pl.BlockSpec((tm, width), lambda i: (jnp.maximum(i - n_ctx_tiles, 0) + off_b, 0))]


def _pick_x(xa_ref, xb_ref, tm):
    return jnp.where(pl.program_id(0) < N_CTX // tm, xa_ref[...], xb_ref[...])


def _inproj_body(xa_ref, xb_ref, mod_ref, n_ref, w_ref, o_ref, *, tm):
    h = _norm_mod(_pick_x(xa_ref, xb_ref, tm), n_ref[...], mod_ref[1:2, :], mod_ref[0:1, :])
    o_ref[...] = _dot(h.astype(BF16), w_ref[...]).astype(BF16)


def _inproj(x, mod, n1, w_in_b):
    tm = 512
    xa, xb, off_b = _split_x(x, tm)
    return pl.pallas_call(
        functools.partial(_inproj_body, tm=tm),
        grid=(N_TOK // tm,),
        in_specs=_x_specs(tm, off_b, D) + [
            pl.BlockSpec((None, 6, D), lambda i: (_mod_row(i, tm), 0, 0)),
            pl.BlockSpec((1, D), lambda i: (0, 0)),
            pl.BlockSpec((D, Z_W), lambda i: (0, 0), pipeline_mode=pl.Buffered(1))],
        out_specs=pl.BlockSpec((tm, Z_W), lambda i: (i, 0)),
        out_shape=jax.ShapeDtypeStruct((N_TOK, Z_W), BF16),
        compiler_params=_cparams(("arbitrary",)),
        name="inproj",
    )(xa, xb, mod, n1.reshape(1, D), w_in_b)


def _gla_body(qk_ref, v_ref, gg_ref, lr_ref, wlr_ref, blr_ref, s0_ref, nw_ref, y_ref, st_ref,
              g_s, o_s, st_s, qt_s, ke_s, dec_s, *, T, U):
    C = GLA_C
    NC = T // C
    RT = U * T
    qw = GLA_H * GLA_DK
    lr = lr_ref[...]
    for d in range(2):
        x = _dot(lr, wlr_ref[d]) + blr_ref[d]
        g_s[d] = (jnp.minimum(x, 0.0) - jnp.log(1.0 + jnp.exp(-jnp.abs(x)))) * (1.0 / GLA_TAU)
    st_s[...] = s0_ref[...]

    R = 128
    cpb = R // C
    ri = lax.broadcasted_iota(jnp.int32, (R, R), 0)
    ci = lax.broadcasted_iota(jnp.int32, (R, R), 1)
    same = (ri // C) == (ci // C)
    masks = (jnp.logical_and(same, ci <= ri), jnp.logical_and(same, ci >= ri))
    sum_ops = tuple(jnp.concatenate([m.astype(BF16), same.astype(BF16)], axis=0) for m in masks)

    def intra(i, carry):
        r0 = pl.multiple_of(i * R, R)
        q = qk_ref[pl.ds(r0, R), 0:qw].astype(F32) * (GLA_DK ** -0.5)
        k = qk_ref[pl.ds(r0, R), qw:2 * qw].astype(F32)
        vb = v_ref[pl.ds(r0, R), :]
        heads = [(d, h) for d in range(2) for h in range(GLA_H)]
        ks = [slice(h * GLA_DK, (h + 1) * GLA_DK) for h in range(GLA_H)]
        sums = []
        for d in range(2):
            g = g_s[d, pl.ds(r0, R), :]
            g_hi = g.astype(BF16)
            r1 = g - g_hi.astype(F32)
            g_mid = r1.astype(BF16)
            g_lo = (r1 - g_mid.astype(F32)).astype(BF16)
            sums.append(_dot(sum_ops[d], jnp.concatenate([g_hi, g_mid, g_lo], axis=1)))
        qts, kts = [], []
        for d in range(2):
            s3 = sums[d][:, 0:qw] + sums[d][:, qw:2 * qw] + sums[d][:, 2 * qw:3 * qw]
            b = s3[0:R]
            bl = s3[R:2 * R]
            qts.append((q * jnp.exp(b)).astype(BF16))
            kts.append((k * jnp.exp(-b)).astype(BF16))
            qt_s[d, pl.ds(r0, R), :] = qts[d]
            ke_s[d, pl.ds(r0, R), :] = (k * jnp.exp(bl - b)).astype(BF16)
            ebl = jnp.exp(bl)
            for j in range(cpb):
                dec_s[d, pl.ds(i * cpb + j, 1), :] = ebl[j * C:j * C + 1, :]
        a = [_dot_nt(qts[d][:, ks[h]], kts[d][:, ks[h]]) for d, h in heads]
        a = [jnp.where(masks[d], a[n], 0.0).astype(BF16) for n, (d, h) in enumerate(heads)]
        o = [_dot(a[n], vb[:, h * GLA_DV:(h + 1) * GLA_DV]) for n, (d, h) in enumerate(heads)]
        for d in range(2):
            o_s[d, pl.ds(r0, R), :] = jnp.concatenate(o[d * GLA_H:(d + 1) * GLA_H], axis=1)
        return carry

    lax.fori_loop(0, RT // R, intra, 0)

    def inter(i, carry):
        ks = [slice(h * GLA_DK, (h + 1) * GLA_DK) for h in range(GLA_H)]
        for u in range(U):
            for d in range(2):
                c = u * NC + (i if d == 0 else NC - 1 - i)
                r0 = pl.multiple_of(c * C, C)
                qt = qt_s[d, pl.ds(r0, C), :]
                ke = ke_s[d, pl.ds(r0, C), :]
                vb = v_ref[pl.ds(r0, C), :]
                dec = dec_s[d, pl.ds(c, 1), :]
                o = []
                for h in range(GLA_H):
                    s_t = st_s[u, d, h]
                    o.append(_dot_nt(qt[:, ks[h]], s_t.astype(BF16)))
                    u_t = _dot_tn(vb[:, h * GLA_DV:(h + 1) * GLA_DV], ke[:, ks[h]])
                    st_s[u, d, h] = s_t * dec[:, ks[h]] + u_t
                o_s[d, pl.ds(r0, C), :] += jnp.concatenate(o, axis=1)
        return carry

    lax.fori_loop(0, NC, inter, 0)

    ch = 128
    nw = nw_ref[...]

    def epi(i, carry):
        r = pl.multiple_of(i * ch, ch)
        for h in range(GLA_H):
            cs = slice(h * GLA_DV, (h + 1) * GLA_DV)
            o = o_s[0, pl.ds(r, ch), cs] + o_s[1, pl.ds(r, ch), cs]
            ms = jnp.mean(o * o, axis=-1, keepdims=True)
            y = (o * lax.rsqrt(ms + EPS) * nw) * _silu(gg_ref[pl.ds(r, ch), cs].astype(F32))
            y_ref[pl.ds(r, ch), cs] = y.astype(BF16)
        return carry

    lax.fori_loop(0, RT // ch, epi, 0)
    st_ref[...] = st_s[...]


GLA_UNITS = {CTX_T: 4, LAT_T: 2}


def _gla(z, wlr, blr, s0_t, nw, *, T, n_units, unit0):
    U = GLA_UNITS[T]
    RT = U * T
    rb = lambda s: s + unit0 * T // RT
    hv = GLA_H * GLA_DV
    qw = GLA_H * GLA_DK
    st_spec = pl.BlockSpec((U, 2, GLA_H, GLA_DV, GLA_DK), lambda s: (s, 0, 0, 0, 0))
    in_specs = [pl.BlockSpec((RT, 2 * qw), lambda s: (rb(s), Z_GQ // (2 * qw))),
                pl.BlockSpec((RT, hv), lambda s: (rb(s), Z_GV // hv)),
                pl.BlockSpec((RT, hv), lambda s: (rb(s), Z_GG // hv)),
                pl.BlockSpec((RT, LANE), lambda s: (rb(s), Z_LR // LANE)),
                pl.BlockSpec((2, LANE, qw), lambda s: (0, 0, 0)),
                pl.BlockSpec((2, 1, qw), lambda s: (0, 0, 0)),
                st_spec,
                pl.BlockSpec((1, GLA_DV), lambda s: (0, 0))]
    args = [z, z, z, z, wlr, blr, s0_t, nw.reshape(1, GLA_DV)]
    return pl.pallas_call(
        functools.partial(_gla_body, T=T, U=U),
        grid=(n_units // U,),
        in_specs=in_specs,
        out_specs=[pl.BlockSpec((RT, hv), lambda s: (s, 0)), st_spec],
        out_shape=[jax.ShapeDtypeStruct((n_units * T, hv), BF16),
                   jax.ShapeDtypeStruct((n_units, 2, GLA_H, GLA_DV, GLA_DK), F32)],
        scratch_shapes=[pltpu.VMEM((2, RT, qw), F32),
                        pltpu.VMEM((2, RT, hv), F32),
                        pltpu.VMEM((U, 2, GLA_H, GLA_DV, GLA_DK), F32),
                        pltpu.VMEM((2, RT, qw), BF16),
                        pltpu.VMEM((2, RT, qw), BF16),
                        pltpu.VMEM((2, RT // GLA_C, qw), F32)],
        compiler_params=_cparams(("arbitrary",)),
        name=f"gla_T{T}",
    )(*args)


def _fft_body(x_ref, w2_ref, dt_ref, y_ref, p_s, *, T):
    xb = x_ref[...]
    for g in range(FFT_G):
        cs = slice(g * FFT_CH, (g + 1) * FFT_CH)
        p = _dot(xb[:, cs], w2_ref[...])
        p_s[0:T, cs] = p[:, :FFT_CH].astype(BF16)
        p_s[T:2 * T, cs] = p[:, FFT_CH:].astype(BF16)
    y_ref[...] = _dot(dt_ref[...], p_s[...]).astype(BF16)


def _dft_consts(T):
    c = np.arange(FFT_CH)
    ang_c = (np.outer(c, c) % FFT_CH) * (2.0 * np.pi / FFT_CH)
    w2 = np.concatenate([np.cos(ang_c), np.sin(ang_c)], axis=1) / np.sqrt(FFT_CH)
    t = np.arange(T)
    ang_t = (np.outer(t, t) % T) * (2.0 * np.pi / T)
    dt = np.concatenate([np.cos(ang_t), -np.sin(ang_t)], axis=1) / np.sqrt(T)
    return jnp.asarray(w2, F32).astype(BF16), jnp.asarray(dt, F32).astype(BF16)


def _fft(z, *, T, n_units, unit0):
    w2, dt = _dft_consts(T)
    fw = FFT_G * FFT_CH
    in_specs = [pl.BlockSpec((T, fw), lambda u: (u + unit0, Z_FF // fw)),
                pl.BlockSpec((FFT_CH, 2 * FFT_CH), lambda u: (0, 0)),
                pl.BlockSpec((T, 2 * T), lambda u: (0, 0))]
    args = [z, w2, dt]
    return pl.pallas_call(
        functools.partial(_fft_body, T=T),
        grid=(n_units,),
        in_specs=in_specs,
        out_specs=pl.BlockSpec((T, fw), lambda u: (u, 0)),
        out_shape=jax.ShapeDtypeStruct((n_units * T, fw), BF16),
        scratch_shapes=[pltpu.VMEM((2 * T, fw), BF16)],
        compiler_params=_cparams(("arbitrary",)),
        name=f"fft_T{T}",
    )(*args)


def _rope_tables(T):
    half = HEAD_DIM // 2
    inv = ROPE_BASE ** (-np.arange(0, half, 2, dtype=np.float64) / half)
    t = np.arange(T)
    ang_r = (t // GRID_W)[:, None] * inv[None, :]
    ang_c = (t % GRID_W)[:, None] * inv[None, :]
    cos = np.concatenate([np.cos(ang_r)] * 2 + [np.cos(ang_c)] * 2, axis=1)
    sin = np.concatenate([-np.sin(ang_r), np.sin(ang_r), -np.sin(ang_c), np.sin(ang_c)], axis=1)
    return jnp.asarray(cos, F32), jnp.asarray(sin, F32)


def _rope(x, cos, sin):
    lane = lax.broadcasted_iota(jnp.int32, x.shape, 1)
    quarter = HEAD_DIM // 4
    partner = jnp.where(lane % (2 * quarter) < quarter,
                        pltpu.roll(x, HEAD_DIM - quarter, 1), pltpu.roll(x, quarter, 1))
    return x * cos + partner * sin


def _rms128(x, w):
    return x * lax.rsqrt(jnp.mean(x * x, axis=-1, keepdims=True) + EPS) * w


def _attn_body(*refs, T, latent):
    if latent:
        (sink_ref, q_ref, k_ref, v_ref, qw_ref, kw_ref, ck_ref, cv_ref, cos_ref, sin_ref, y_ref) = refs
    else:
        (sink_ref, q_ref, k_ref, v_ref, qw_ref, kw_ref, y_ref, ko_ref, vo_ref) = refs
    kv = pl.program_id(1)
    nb = T // ATT_BLOCK
    kn = _rms128(k_ref[...].astype(F32), kw_ref[...])
    vb = v_ref[...]
    if latent:
        kn = _rope(kn, cos_ref[...], sin_ref[...])
        ckb = ck_ref[...].astype(BF16)
        cvb = cv_ref[...].astype(BF16)
    else:
        ko_ref[...] = kn
        vo_ref[...] = vb.astype(F32)
    kb = kn.astype(BF16)
    rows_g = lax.broadcasted_iota(jnp.int32, (GQA * ATT_BLOCK, 1), 0) // ATT_BLOCK
    sink = jnp.zeros((GQA * ATT_BLOCK, 1), F32)
    for g in range(GQA):
        sink = jnp.where(rows_g == g, sink_ref[kv * GQA + g], sink)
    scale = HEAD_DIM ** -0.5
    for n in range(nb):
        rs = slice(n * ATT_BLOCK, (n + 1) * ATT_BLOCK)
        qs = []
        for g in range(GQA):
            qn = _rms128(q_ref[rs, g * HEAD_DIM:(g + 1) * HEAD_DIM].astype(F32), qw_ref[...])
            if latent:
                qn = _rope(qn, cos_ref[rs, :], sin_ref[rs, :])
            qs.append((qn * scale).astype(BF16))
        qq = jnp.concatenate(qs, axis=0)
        if latent:
            lo, hi = max(n - 1, 0), min(n + 2, nb)
            ks = slice(lo * ATT_BLOCK, hi * ATT_BLOCK)
            w = (hi - lo) * ATT_BLOCK
            s_loc = _dot_nt(qq, kb[ks])
            qpos = n * ATT_BLOCK + lax.broadcasted_iota(jnp.int32, (GQA * ATT_BLOCK, w), 0) % ATT_BLOCK
            kpos = lo * ATT_BLOCK + lax.broadcasted_iota(jnp.int32, (GQA * ATT_BLOCK, w), 1)
            s_loc = jnp.where(jnp.abs(kpos - qpos) <= WINDOW, s_loc, NEG_INF)
            s_ctx = _dot_nt(qq, ckb)
            m = jnp.maximum(jnp.maximum(jnp.max(s_loc, axis=-1, keepdims=True),
                                        jnp.max(s_ctx, axis=-1, keepdims=True)), sink)
            p_loc = jnp.exp(s_loc - m)
            p_ctx = jnp.exp(s_ctx - m)
            den = (jnp.sum(p_loc, axis=-1, keepdims=True) + jnp.sum(p_ctx, axis=-1, keepdims=True)
                   + jnp.exp(sink - m))
            o = _dot(p_loc.astype(BF16), vb[ks]) + _dot(p_ctx.astype(BF16), cvb)
        else:
            s = _dot_nt(qq, kb)
            m = jnp.maximum(jnp.max(s, axis=-1, keepdims=True), sink)
            p = jnp.exp(s - m)
            den = jnp.sum(p, axis=-1, keepdims=True) + jnp.exp(sink - m)
            o = _dot(p.astype(BF16), vb)
        o = o * (1.0 / den)
        for g in range(GQA):
            y_ref[rs, g * HEAD_DIM:(g + 1) * HEAD_DIM] = o[g * ATT_BLOCK:(g + 1) * ATT_BLOCK].astype(BF16)


def _attn(z, sink_l, qw, kw, *, T, n_units, unit0, ctx_k=None, ctx_v=None):
    latent = ctx_k is not None
    qwid = GQA * HEAD_DIM
    rb = lambda u: u + unit0
    smem = pl.BlockSpec(memory_space=pltpu.SMEM)
    in_specs = [smem,
                pl.BlockSpec((T, qwid), lambda u, h: (rb(u), Z_AQ // qwid + h)),
                pl.BlockSpec((T, HEAD_DIM), lambda u, h: (rb(u), Z_AK // HEAD_DIM + h)),
                pl.BlockSpec((T, HEAD_DIM), lambda u, h: (rb(u), Z_AV // HEAD_DIM + h)),
                pl.BlockSpec((1, HEAD_DIM), lambda u, h: (0, 0)),
                pl.BlockSpec((1, HEAD_DIM), lambda u, h: (0, 0))]
    args = [sink_l, z, z, z, qw.reshape(1, HEAD_DIM), kw.reshape(1, HEAD_DIM)]
    y_spec = pl.BlockSpec((T, qwid), lambda u, h: (u, h))
    y_shape = jax.ShapeDtypeStruct((n_units * T, N_HEADS * HEAD_DIM), BF16)
    if latent:
        cos, sin = _rope_tables(T)
        in_specs += [pl.BlockSpec((None, PAST, HEAD_DIM), lambda u, h: (u, 0, h)),
                     pl.BlockSpec((None, PAST, HEAD_DIM), lambda u, h: (u, 0, h)),
                     pl.BlockSpec((T, HEAD_DIM), lambda u, h: (0, 0)),
                     pl.BlockSpec((T, HEAD_DIM), lambda u, h: (0, 0))]
        args += [ctx_k, ctx_v, cos, sin]
        out_specs, out_shape = y_spec, y_shape
    else:
        kv_spec = pl.BlockSpec((T, HEAD_DIM), lambda u, h: (u, h))
        kv_shape = jax.ShapeDtypeStruct((n_units * T, N_KV * HEAD_DIM), F32)
        out_specs, out_shape = [y_spec, kv_spec, kv_spec], [y_shape, kv_shape, kv_shape]
    return pl.pallas_call(
        functools.partial(_attn_body, T=T, latent=latent),
        grid=(n_units, N_KV),
        in_specs=in_specs,
        out_specs=out_specs,
        out_shape=out_shape,
        compiler_params=_cparams(("arbitrary", "arbitrary")),
        name=f"attn_T{T}",
    )(*args)


def _outproj_body(*refs, router, tm):
    y_refs, refs = refs[:6], refs[6:]
    if router:
        (w_ref, xa_ref, xb_ref, mod_ref, n2_ref, r_ref, x1_ref, h2_ref, idx_ref, gate_ref, cnt_ref, cnt_s) = refs
    else:
        (w_ref, xa_ref, xb_ref, mod_ref, n2_ref, x1_ref, h2_ref) = refs
    gw = GLA_H * GLA_DV
    fw = FFT_G * FFT_CH
    yg, yf, ya = (_pick_x(y_refs[2 * j], y_refs[2 * j + 1], tm) for j in range(3))
    acc = _dot(yg, w_ref[0:gw, :]) + _dot(yf, w_ref[gw:gw + fw, :]) + _dot(ya, w_ref[gw + fw:, :])
    x1 = _pick_x(xa_ref, xb_ref, tm) + mod_ref[2:3, :] * acc
    x1_ref[...] = x1
    h2 = _norm_mod(x1, n2_ref[...], mod_ref[4:5, :], mod_ref[3:4, :])
    h2_ref[...] = h2.astype(h2_ref.dtype)
    if router:
        logits = _dot(h2.astype(BF16), r_ref[...])
        lane = lax.broadcasted_iota(jnp.int32, logits.shape, 1)
        lg = jnp.where(lane < N_EXP, logits, -jnp.inf)
        v1 = jnp.max(lg, axis=-1, keepdims=True)
        i1 = jnp.min(jnp.where(lg == v1, lane, LANE), axis=-1, keepdims=True)
        lg2 = jnp.where(lane == i1, -jnp.inf, lg)
        v2 = jnp.max(lg2, axis=-1, keepdims=True)
        i2 = jnp.min(jnp.where(lg2 == v2, lane, LANE), axis=-1, keepdims=True)
        e = jnp.exp(v2 - v1)
        g1 = 1.0 / (1.0 + e)
        gate_ref[...] = jnp.where(lane == 0, g1, jnp.where(lane == 1, e * g1, 0.0))

        @pl.when(pl.program_id(0) == 0)
        def _():
            cnt_s[...] = jnp.zeros_like(cnt_s)

        oh1 = lane == i1
        oh2 = lane == i2
        oh = jnp.where(jnp.logical_or(oh1, oh2), 1.0, 0.0)
        ri = lax.broadcasted_iota(jnp.int32, (tm, tm), 0)
        ci = lax.broadcasted_iota(jnp.int32, (tm, tm), 1)
        before = jnp.where(ci < ri, 1.0, 0.0).astype(BF16)
        prior = _dot(before, oh.astype(BF16)) + cnt_s[0:1, :]
        r1 = jnp.sum(jnp.where(oh1, prior, 0.0), axis=-1, keepdims=True).astype(jnp.int32)
        r2 = jnp.sum(jnp.where(oh2, prior, 0.0), axis=-1, keepdims=True).astype(jnp.int32)
        idx_ref[...] = jnp.where(lane == 0, i1, jnp.where(lane == 1, i2,
                                 jnp.where(lane == 2, r1, jnp.where(lane == 3, r2, 0))))
        cnt_s[...] = cnt_s[...] + jnp.sum(oh, axis=0, keepdims=True)
        cnt_ref[...] = cnt_s[...]


def _outproj(y_gla, y_fft, y_att, w_out_b, x, mod, n2, router_b=None):
    tm = 512
    router = router_b is not None
    gw, fw, aw = GLA_H * GLA_DV, FFT_G * FFT_CH, N_HEADS * HEAD_DIM
    row = lambda i: (i, 0)
    xa, xb, off_b = _split_x(x, tm)
    in_specs, args = [], []
    for pair, wid in ((y_gla, gw), (y_fft, fw), (y_att, aw)):
        in_specs += _x_specs(tm, 0, wid)
        args += list(pair)
    in_specs.append(pl.BlockSpec((gw + fw + aw, D), lambda i: (0, 0), pipeline_mode=pl.Buffered(1)))
    in_specs += _x_specs(tm, off_b, D)
    in_specs += [pl.BlockSpec((None, 6, D), lambda i: (_mod_row(i, tm), 0, 0)),
                 pl.BlockSpec((1, D), lambda i: (0, 0))]
    args += [w_out_b, xa, xb, mod, n2.reshape(1, D)]
    out_specs = [pl.BlockSpec((tm, D), row), pl.BlockSpec((tm, D), row)]
    out_shape = [jax.ShapeDtypeStruct((N_TOK, D), F32),
                 jax.ShapeDtypeStruct((N_TOK, D), F32 if router else BF16)]
    if router:
        in_specs.append(pl.BlockSpec((D, LANE), lambda i: (0, 0)))
        args.append(router_b)
        out_specs += [pl.BlockSpec((tm, LANE), row), pl.BlockSpec((tm, LANE), row),
                      pl.BlockSpec((8, LANE), lambda i: (0, 0))]
        out_shape += [jax.ShapeDtypeStruct((N_TOK, LANE), jnp.int32), jax.ShapeDtypeStruct((N_TOK, LANE), F32),
                      jax.ShapeDtypeStruct((8, LANE), F32)]
    return pl.pallas_call(
        functools.partial(_outproj_body, router=router, tm=tm),
        grid=(N_TOK // tm,),
        in_specs=in_specs,
        out_specs=out_specs,
        out_shape=out_shape,
        scratch_shapes=[pltpu.VMEM((8, LANE), F32)] if router else [],
        compiler_params=_cparams(("arbitrary",)),
        name="outproj_router" if router else "outproj",
    )(*args)


FF_TF = 512
FF_KA = D_FF // FF_TF


def _store_f_slice(g_s, g, s, rows=slice(None)):
    for kk in range(FF_KA):
        @pl.when(s == kk)
        def _(kk=kk):
            g_s[rows, kk * FF_TF:(kk + 1) * FF_TF] = g


def _ffn_body(h_ref, w1_ref, w3_ref, w2_ref, x_ref, mod_ref, o_ref, g_s):
    s = pl.program_id(1)

    @pl.when(s < FF_KA)
    def _():
        h = h_ref[...]
        g = (_silu(_dot(h, w1_ref[...])) * _dot(h, w3_ref[...])).astype(BF16)
        _store_f_slice(g_s, g, s)

    @pl.when(s >= FF_KA)
    def _():
        o_ref[...] = x_ref[...] + mod_ref[5:6, :] * _dot(g_s[...], w2_ref[...])


def _ffn(h2, w1_b, w3_b, w2_b, x1, mod):
    tm, tn = 1024, 512
    kb = D // tn
    up = lambda i, s: (0, jnp.minimum(s, FF_KA - 1))
    down = lambda s: jnp.maximum(s - FF_KA, 0)
    return pl.pallas_call(
        _ffn_body,
        grid=(N_TOK // tm, FF_KA + kb),
        in_specs=[pl.BlockSpec((tm, D), lambda i, s: (i, 0)),
                  pl.BlockSpec((D, FF_TF), up),
                  pl.BlockSpec((D, FF_TF), up),
                  pl.BlockSpec((D_FF, tn), lambda i, s: (0, down(s))),
                  pl.BlockSpec((tm, tn), lambda i, s: (i, down(s))),
                  pl.BlockSpec((None, 6, tn), lambda i, s: (_mod_row(i, tm), 0, down(s)))],
        out_specs=pl.BlockSpec((tm, tn), lambda i, s: (i, down(s))),
        out_shape=jax.ShapeDtypeStruct((N_TOK, D), F32),
        scratch_shapes=[pltpu.VMEM((tm, D_FF), BF16)],
        compiler_params=_cparams(("arbitrary", "arbitrary")),
        name="ffn_dense",
    )(h2, w1_b, w3_b, w2_b, x1, mod)


MOE_TN = 256
MOE_KB = D // MOE_TN
MOE_GATHER = 256


def _moe_body(te_ref, tr_ref, src_ref, h_hbm, w1_ref, w3_ref, w2_ref, y_ref, xg_s, xb_s, g_s, sem):
    i = pl.program_id(0)
    s = pl.program_id(1)
    n_rows = tr_ref[i]
    parts = MOE_TM // MOE_GATHER
    cases = tuple((slice(0, q * MOE_GATHER),
                   jnp.logical_and(n_rows > (q - 1) * MOE_GATHER, n_rows <= q * MOE_GATHER))
                  for q in range(1, parts + 1))

    def row_copy(r, t):
        return pltpu.make_async_copy(h_hbm.at[pl.ds(t, 1)], xg_s.at[pl.ds(r, 1)], sem)

    for p in range(parts):
        @pl.when(jnp.logical_and(n_rows > p * MOE_GATHER, s == 0))
        def _(p=p):
            base = i * MOE_TM + p * MOE_GATHER

            def issue(r, c):
                row_copy(r, src_ref[base + r]).start()
                return c

            lax.fori_loop(0, MOE_GATHER, issue, 0, unroll=8)
            pltpu.make_async_copy(h_hbm.at[pl.ds(0, MOE_GATHER)], xg_s, sem).wait()
            xb_s[p * MOE_GATHER:(p + 1) * MOE_GATHER, :] = xg_s[...].astype(BF16)

    for rows, cond in cases:
        @pl.when(jnp.logical_and(cond, s < FF_KA))
        def _(rows=rows):
            h = xb_s[rows, :]
            g = (_silu(_dot(h, w1_ref[...].astype(BF16))) * _dot(h, w3_ref[...].astype(BF16))).astype(BF16)
            _store_f_slice(g_s, g, s, rows)

        @pl.when(jnp.logical_and(cond, s >= FF_KA))
        def _(rows=rows):
            y_ref[rows, :] = _dot(g_s[rows, :], w2_ref[...].astype(BF16))

    for p in range(parts):
        @pl.when(n_rows <= p * MOE_GATHER)
        def _(p=p):
            y_ref[p * MOE_GATHER:(p + 1) * MOE_GATHER, :] = jnp.zeros((MOE_GATHER, MOE_TN), F32)


def _moe(tile_exp, tile_rows, row_src, h2, w1, w3, w2):
    def up(i, s, te, tr, src):
        return (te[i], 0, jnp.where(tr[i] > 0, jnp.minimum(s, FF_KA - 1), FF_KA - 1))

    def down(i, s, te, tr, src):
        return (te[i], 0, jnp.where(tr[i] > 0, jnp.maximum(s - FF_KA, 0), MOE_KB - 1))

    return pl.pallas_call(
        _moe_body,
        grid_spec=pltpu.PrefetchScalarGridSpec(
            num_scalar_prefetch=3,
            grid=(MOE_TILES, FF_KA + MOE_KB),
            in_specs=[pl.BlockSpec(memory_space=pl.ANY),
                      pl.BlockSpec((None, D, FF_TF), up),
                      pl.BlockSpec((None, D, FF_TF), up),
                      pl.BlockSpec((None, D_FF, MOE_TN), down)],
            out_specs=pl.BlockSpec((MOE_TM, MOE_TN), lambda i, s, te, tr, src: (i, jnp.maximum(s - FF_KA, 0))),
            scratch_shapes=[pltpu.VMEM((MOE_GATHER, D), F32), pltpu.VMEM((MOE_TM, D), BF16),
                            pltpu.VMEM((MOE_TM, D_FF), BF16), pltpu.SemaphoreType.DMA(())],
        ),
        out_shape=jax.ShapeDtypeStruct((MOE_ROWS, D), F32),
        compiler_params=_cparams(("arbitrary", "arbitrary")),
        name="moe_experts",
    )(tile_exp, tile_rows, row_src, h2, w1, w3, w2)


def _combine_body(slot_ref, y_hbm, x_ref, g_ref, mod_ref, oc_ref, ol_ref, buf, sems, *, tc):
    i = pl.program_id(0)

    def gather(tile, b):
        base = tile * tc

        def issue(r, c):
            for k in range(2):
                pltpu.make_async_copy(y_hbm.at[pl.ds(slot_ref[2 * (base + r) + k], 1)],
                                      buf.at[b, k, pl.ds(r, 1)], sems.at[b]).start()
            return c

        lax.fori_loop(0, tc, issue, 0, unroll=8)

    def wait(b):
        for k in range(2):
            pltpu.make_async_copy(y_hbm.at[pl.ds(0, tc)], buf.at[b, k], sems.at[b]).wait()

    @pl.when(i == 0)
    def _():
        gather(0, 0)

    for b in range(2):
        @pl.when(jnp.logical_and(i % 2 == b, i + 1 < pl.num_programs(0)))
        def _(b=b):
            gather(i + 1, 1 - b)

    for b in range(2):
        @pl.when(i % 2 == b)
        def _(b=b):
            wait(b)
            g = g_ref[...]
            f = g[:, 0:1] * buf[b, 0] + g[:, 1:2] * buf[b, 1]
            _combine_store(x_ref[...] + mod_ref[5:6, :] * f, i, oc_ref, ol_ref, tc)


def _combine_store(out, i, oc_ref, ol_ref, tc):
    is_ctx = i < N_CTX // tc

    @pl.when(is_ctx)
    def _():
        oc_ref[...] = out

    @pl.when(jnp.logical_not(is_ctx))
    def _():
        ol_ref[...] = out


def _combine(slot, y, x1, gates, mod):
    tc = 256
    n_ctx_tiles = N_CTX // tc
    return pl.pallas_call(
        functools.partial(_combine_body, tc=tc),
        grid_spec=pltpu.PrefetchScalarGridSpec(
            num_scalar_prefetch=1,
            grid=(N_TOK // tc,),
            in_specs=[pl.BlockSpec(memory_space=pl.ANY),
                      pl.BlockSpec((tc, D), lambda i, s: (i, 0)),
                      pl.BlockSpec((tc, LANE), lambda i, s: (i, 0)),
                      pl.BlockSpec((None, 6, D), lambda i, s: (_mod_row(i, tc), 0, 0))],
            out_specs=[pl.BlockSpec((tc, D), lambda i, s: (jnp.minimum(i, n_ctx_tiles - 1), 0)),
                       pl.BlockSpec((tc, D), lambda i, s: (jnp.maximum(i - n_ctx_tiles, 0), 0))],
            scratch_shapes=[pltpu.VMEM((2, 2, tc, D), F32), pltpu.SemaphoreType.DMA((2,))],
        ),
        out_shape=[jax.ShapeDtypeStruct((N_CTX, D), F32), jax.ShapeDtypeStruct((N_LAT, D), F32)],
        compiler_params=_cparams(("arbitrary",)),
        name="moe_combine",
    )(slot, y, x1, gates, mod)


def _route_meta(idx4, counts):
    counts = counts.astype(jnp.int32)
    tiles = (counts + MOE_TM - 1) // MOE_TM
    tile_end = jnp.cumsum(tiles)
    tile_start = tile_end - tiles
    n_tiles = tile_end[-1]
    exp_of = idx4[:, 0:2].reshape(-1)
    rank = idx4[:, 2:4].reshape(-1)
    first_row = jnp.sum(jnp.where(exp_of[:, None] == jnp.arange(N_EXP, dtype=jnp.int32)[None, :],
                                  (tile_start * MOE_TM)[None, :], 0), axis=1)
    slot = (first_row + rank).astype(jnp.int32)
    j = jnp.arange(MOE_TILES, dtype=jnp.int32)
    te = jnp.sum((j[:, None] >= tile_end[None, :]).astype(jnp.int32), axis=1)
    te = jnp.where(j < n_tiles, te, te[jnp.maximum(n_tiles - 1, 0)])
    te = jnp.minimum(te, N_EXP - 1).astype(jnp.int32)
    rows = jnp.clip(counts[te] - (j - tile_start[te]) * MOE_TM, 0, MOE_TM)
    rows = jnp.where(j < n_tiles, rows, 0).astype(jnp.int32)
    tok = jnp.arange(2 * N_TOK, dtype=jnp.int32) // 2
    row_src = jnp.zeros((MOE_ROWS,), jnp.int32).at[slot].set(tok, unique_indices=True)
    return te, rows, row_src, slot


IN_W = 3616
LR_AT = Z_FF


def _prep_w_in_body(w_ref, o_ref):
    lr_w = 2 * GLA_RANK
    rows = w_ref.shape[0]
    o_ref[:, 0:LR_AT] = w_ref[:, 0:LR_AT].astype(BF16)
    o_ref[:, LR_AT:Z_LR] = w_ref[:, LR_AT + lr_w:IN_W].astype(BF16)
    tail = jnp.concatenate([w_ref[:, LR_AT:LR_AT + lr_w], jnp.zeros((rows, LANE - lr_w), F32)], axis=1)
    o_ref[:, Z_LR:Z_LR + LANE] = tail.astype(BF16)
    o_ref[:, Z_LR + LANE:] = jnp.zeros((rows, Z_W - Z_LR - LANE), BF16)


def _prep_w_in(w):
    tr = 256
    return pl.pallas_call(
        _prep_w_in_body,
        grid=(D // tr,),
        in_specs=[pl.BlockSpec((tr, IN_W), lambda i: (i, 0))],
        out_specs=pl.BlockSpec((tr, Z_W), lambda i: (i, 0)),
        out_shape=jax.ShapeDtypeStruct((D, Z_W), BF16),
        compiler_params=_cparams(("arbitrary",)),
        name="prep_w_in",
    )(w)


def _prep_w_lr(w_lr2):
    out = jnp.zeros((2, LANE, GLA_H * GLA_DK), F32)
    out = out.at[0, 0:GLA_RANK].set(w_lr2[0]).at[1, GLA_RANK:2 * GLA_RANK].set(w_lr2[1])
    return out.astype(BF16)


def kernel(x_prompt, x_sample, cache_k, cache_v, state_gla, c, c_ctx, w_ada, b_ada, norm1_w, norm2_w, w_in, w_gla_lr2, b_gla_lr2, gla_norm_w, q_norm_w, k_norm_w, attn_sink, w_out, ffn_w1, ffn_w3, ffn_w2, moe_router, moe_w1, moe_w3, moe_w2):
    x = (x_prompt.reshape(N_CTX, D), x_sample.reshape(N_LAT, D))
    cond16 = jnp.concatenate([c_ctx[None, :], c, jnp.zeros((16 - 1 - N_LAT_B, D), F32)], axis=0)
    mods = _mods(cond16, w_ada, b_ada).reshape(DEPTH, 16, 6, D)
    ck_all = cache_k.reshape(N_LAT_B, DEPTH, PAST, N_KV * HEAD_DIM)
    cv_all = cache_v.reshape(N_LAT_B, DEPTH, PAST, N_KV * HEAD_DIM)
    s0_all = jnp.swapaxes(state_gla, -1, -2)
    zero_state = jnp.zeros((N_CTX_B, 2, GLA_H, GLA_DV, GLA_DK), F32)
    ctx_units = dict(T=CTX_T, n_units=N_CTX_B, unit0=0)
    lat_units = dict(T=LAT_T, n_units=N_LAT_B, unit0=N_CTX // LAT_T)

    ks, vs, ss = [], [], []
    for l in range(DEPTH):
        mod = mods[l]
        z = _inproj(x, mod, norm1_w[l], _prep_w_in(w_in[l]))
        wlr = _prep_w_lr(w_gla_lr2[l])
        blr = b_gla_lr2[l].reshape(2, 1, GLA_H * GLA_DK)
        yg_c, st_c = _gla(z, wlr, blr, zero_state, gla_norm_w[l], **ctx_units)
        yg_l, _ = _gla(z, wlr, blr, s0_all[:, l], gla_norm_w[l], **lat_units)
        y_gla = (yg_c, yg_l)
        y_fft = (_fft(z, **ctx_units), _fft(z, **lat_units))
        ya_c, k_c, v_c = _attn(z, attn_sink[l], q_norm_w[l], k_norm_w[l], **ctx_units)
        ya_l = _attn(z, attn_sink[l], q_norm_w[l], k_norm_w[l], ctx_k=ck_all[:, l], ctx_v=cv_all[:, l],
                     **lat_units)
        y_att = (ya_c, ya_l)
        w_out_b = w_out[l].astype(BF16)
        e = l // 2
        if l % 2 == 0:
            x1, h2 = _outproj(y_gla, y_fft, y_att, w_out_b, x, mod, norm2_w[l])
            x = _ffn(h2, ffn_w1[e].astype(BF16), ffn_w3[e].astype(BF16), ffn_w2[e].astype(BF16), x1, mod)
        else:
            router_b = jnp.pad(moe_router[e], ((0, 0), (0, LANE - N_EXP))).astype(BF16)
            x1, h2, idx, gates, counts = _outproj(y_gla, y_fft, y_att, w_out_b, x, mod, norm2_w[l], router_b)
            te, tile_rows, row_src, slot = _route_meta(idx[:, :4], counts[0, :N_EXP])
            y = _moe(te, tile_rows, row_src, h2, moe_w1[e], moe_w3[e], moe_w2[e])
            x = _combine(slot, y, x1, gates, mod)
        ks.append(k_c.reshape(N_CTX_B, CTX_T, N_KV, HEAD_DIM))
        vs.append(v_c.reshape(N_CTX_B, CTX_T, N_KV, HEAD_DIM))
        ss.append(jnp.swapaxes(st_c, -1, -2))
    if not isinstance(x, (tuple, list)):
        x = (x[:N_CTX], x[N_CTX:])
    y_prompt = x[0].reshape(N_CTX_B, CTX_T, D)
    y_sample = x[1].reshape(N_LAT_B, LAT_T, D)
    return (y_prompt, y_sample, jnp.stack(ks, axis=1), jnp.stack(vs, axis=1), jnp.stack(ss, axis=1))
```

```python
import functools

import numpy as np
import jax
import jax.numpy as jnp
from jax import lax
from jax.experimental import pallas as pl
from jax.experimental.pallas import tpu as pltpu

F32 = jnp.float32
BF16 = jnp.bfloat16

D = 2048
N_CTX_B, CTX_T = 32, 256
N_LAT_B, LAT_T = 8, 1024
N_CTX = N_CTX_B * CTX_T
N_LAT = N_LAT_B * LAT_T
N_TOK = N_CTX + N_LAT
DEPTH = 2
PAST = 512
GRID_W = 64
GLA_H, GLA_DK, GLA_DV, GLA_RANK, GLA_TAU, GLA_C = 4, 64, 128, 16, 16.0, 32
FFT_G, FFT_CH = 4, 128
N_HEADS, N_KV, HEAD_DIM = 8, 2, 128
GQA = N_HEADS // N_KV
WINDOW, ATT_BLOCK = 128, 128
ROPE_BASE = 10000.0
D_FF = 5632
N_EXP = 8
EPS = 1e-6
NEG_INF = -1e30

Z_GQ, Z_GK, Z_GV, Z_GG, Z_FF, Z_AQ, Z_AK, Z_AV, Z_LR = 0, 256, 512, 1024, 1536, 2048, 3072, 3328, 3584
Z_W = 3840
LANE = 128

VMEM_LIMIT = 56 * 1024 * 1024

MOE_TM = 1024
MOE_TILES = 2 * N_TOK // MOE_TM + N_EXP
MOE_ROWS = MOE_TILES * MOE_TM


def _cparams(sem):
    return pltpu.CompilerParams(dimension_semantics=sem, vmem_limit_bytes=VMEM_LIMIT)


def _silu(x):
    return x / (1.0 + jnp.exp(-x))


def _norm_mod(x, nw, scale, shift):
    ms = jnp.mean(x * x, axis=-1, keepdims=True)
    return (x * lax.rsqrt(ms + EPS) * nw) * (1.0 + scale) + shift


def _dot(a, b):
    return jnp.dot(a, b, preferred_element_type=F32)


def _dot_nt(a, b):
    return lax.dot_general(a, b, (((1,), (1,)), ((), ())), preferred_element_type=F32)


def _dot_tn(a, b):
    return lax.dot_general(a, b, (((0,), (0,)), ((), ())), preferred_element_type=F32)


def _mod_body(c_ref, w_ref, b_ref, o_ref):
    s = _silu(c_ref[...])
    o_ref[...] = _dot(s.astype(BF16), w_ref[...].astype(BF16)) + b_ref[...]


def _mods(cond16, w_ada, b_ada):
    tn = 1024
    return pl.pallas_call(
        _mod_body,
        grid=(DEPTH, 6 * D // tn),
        in_specs=[pl.BlockSpec((16, D), lambda l, j: (0, 0)),
                  pl.BlockSpec((None, D, tn), lambda l, j: (l, 0, j)),
                  pl.BlockSpec((None, 1, tn), lambda l, j: (l, 0, j))],
        out_specs=pl.BlockSpec((None, 16, tn), lambda l, j: (l, 0, j)),
        out_shape=jax.ShapeDtypeStruct((DEPTH, 16, 6 * D), F32),
        compiler_params=_cparams(("arbitrary", "arbitrary")),
        name="adaln_mod",
    )(cond16, w_ada, b_ada.reshape(DEPTH, 1, 6 * D))


def _mod_row(i, tm):
    per_lat = LAT_T // tm
    n_ctx_tiles = N_CTX // tm
    return jnp.where(i < n_ctx_tiles, 0, 1 + (i - n_ctx_tiles) // per_lat)


def _split_x(x, tm):
    if isinstance(x, (tuple, list)):
        return x[0], x[1], 0
    return x, x, N_CTX // tm


def _x_specs(tm, off_b, width):
    n_ctx_tiles = N_CTX // tm
    return [pl.BlockSpec((tm, width), lambda i: (jnp.minimum(i, n_ctx_tiles - 1), 0)),
            pl.BlockSpec((tm, width), lambda i: (jnp.maximum(i - n_ctx_tiles, 0) + off_b, 0))]


def _pick_x(xa_ref, xb_ref, tm):
    return jnp.where(pl.program_id(0) < N_CTX // tm, xa_ref[...], xb_ref[...])


def _inproj_body(xa_ref, xb_ref, mod_ref, n_ref, w_ref, o_ref, *, tm):
    h = _norm_mod(_pick_x(xa_ref, xb_ref, tm), n_ref[...], mod_ref[1:2, :], mod_ref[0:1, :])
    o_ref[...] = _dot(h.astype(BF16), w_ref[...]).astype(BF16)


def _inproj(x, mod, n1, w_in_b):
    tm = 512
    xa, xb, off_b = _split_x(x, tm)
    return pl.pallas_call(
        functools.partial(_inproj_body, tm=tm),
        grid=(N_TOK // tm,),
        in_specs=_x_specs(tm, off_b, D) + [
            pl.BlockSpec((None, 6, D), lambda i: (_mod_row(i, tm), 0, 0)),
            pl.BlockSpec((1, D), lambda i: (0, 0)),
            pl.BlockSpec((D, Z_W), lambda i: (0, 0), pipeline_mode=pl.Buffered(1))],
        out_specs=pl.BlockSpec((tm, Z_W), lambda i: (i, 0)),
        out_shape=jax.ShapeDtypeStruct((N_TOK, Z_W), BF16),
        compiler_params=_cparams(("arbitrary",)),
        name="inproj",
    )(xa, xb, mod, n1.reshape(1, D), w_in_b)


def _gla_body(qk_ref, v_ref, gg_ref, lr_ref, wlr_ref, blr_ref, s0_ref, nw_ref, y_ref, st_ref,
              g_s, o_s, st_s, qt_s, ke_s, dec_s, *, T, U):
    C = GLA_C
    NC = T // C
    RT = U * T
    qw = GLA_H * GLA_DK
    lr = lr_ref[...]
    for d in range(2):
        x = _dot(lr, wlr_ref[d]) + blr_ref[d]
        g_s[d] = (jnp.minimum(x, 0.0) - jnp.log(1.0 + jnp.exp(-jnp.abs(x)))) * (1.0 / GLA_TAU)
    st_s[...] = s0_ref[...]

    R = 128
    cpb = R // C
    ri = lax.broadcasted_iota(jnp.int32, (R, R), 0)
    ci = lax.broadcasted_iota(jnp.int32, (R, R), 1)
    same = (ri // C) == (ci // C)
    masks = (jnp.logical_and(same, ci <= ri), jnp.logical_and(same, ci >= ri))
    sum_ops = tuple(jnp.concatenate([m.astype(BF16), same.astype(BF16)], axis=0) for m in masks)

    def intra(i, carry):
        r0 = pl.multiple_of(i * R, R)
        q = qk_ref[pl.ds(r0, R), 0:qw].astype(F32) * (GLA_DK ** -0.5)
        k = qk_ref[pl.ds(r0, R), qw:2 * qw].astype(F32)
        vb = v_ref[pl.ds(r0, R), :]
        heads = [(d, h) for d in range(2) for h in range(GLA_H)]
        ks = [slice(h * GLA_DK, (h + 1) * GLA_DK) for h in range(GLA_H)]
        sums = []
        for d in range(2):
            g = g_s[d, pl.ds(r0, R), :]
            g_hi = g.astype(BF16)
            r1 = g - g_hi.astype(F32)
            g_mid = r1.astype(BF16)
            g_lo = (r1 - g_mid.astype(F32)).astype(BF16)
            sums.append(_dot(sum_ops[d], jnp.concatenate([g_hi, g_mid, g_lo], axis=1)))
        qts, kts = [], []
        for d in range(2):
            s3 = sums[d][:, 0:qw] + sums[d][:, qw:2 * qw] + sums[d][:, 2 * qw:3 * qw]
            b = s3[0:R]
            bl = s3[R:2 * R]
            qts.append((q * jnp.exp(b)).astype(BF16))
            kts.append((k * jnp.exp(-b)).astype(BF16))
            qt_s[d, pl.ds(r0, R), :] = qts[d]
            ke_s[d, pl.ds(r0, R), :] = (k * jnp.exp(bl - b)).astype(BF16)
            ebl = jnp.exp(bl)
            for j in range(cpb):
                dec_s[d, pl.ds(i * cpb + j, 1), :] = ebl[j * C:j * C + 1, :]
        a = [_dot_nt(qts[d][:, ks[h]], kts[d][:, ks[h]]) for d, h in heads]
        a = [jnp.where(masks[d], a[n], 0.0).astype(BF16) for n, (d, h) in enumerate(heads)]
        o = [_dot(a[n], vb[:, h * GLA_DV:(h + 1) * GLA_DV]) for n, (d, h) in enumerate(heads)]
        for d in range(2):
            o_s[d, pl.ds(r0, R), :] = jnp.concatenate(o[d * GLA_H:(d + 1) * GLA_H], axis=1)
        return carry

    lax.fori_loop(0, RT // R, intra, 0)

    def inter(i, carry):
        ks = [slice(h * GLA_DK, (h + 1) * GLA_DK) for h in range(GLA_H)]
        for u in range(U):
            for d in range(2):
                c = u * NC + (i if d == 0 else NC - 1 - i)
                r0 = pl.multiple_of(c * C, C)
                qt = qt_s[d, pl.ds(r0, C), :]
                ke = ke_s[d, pl.ds(r0, C), :]
                vb = v_ref[pl.ds(r0, C), :]
                dec = dec_s[d, pl.ds(c, 1), :]
                o = []
                for h in range(GLA_H):
                    s_t = st_s[u, d, h]
                    o.append(_dot_nt(qt[:, ks[h]], s_t.astype(BF16)))
                    u_t = _dot_tn(vb[:, h * GLA_DV:(h + 1) * GLA_DV], ke[:, ks[h]])
                    st_s[u, d, h] = s_t * dec[:, ks[h]] + u_t
                o_s[d, pl.ds(r0, C), :] += jnp.concatenate(o, axis=1)
        return carry

    lax.fori_loop(0, NC, inter, 0)

    ch = 128
    nw = nw_ref[...]

    def epi(i, carry):
        r = pl.multiple_of(i * ch, ch)
        for h in range(GLA_H):
            cs = slice(h * GLA_DV, (h + 1) * GLA_DV)
            o = o_s[0, pl.ds(r, ch), cs] + o_s[1, pl.ds(r, ch), cs]
            ms = jnp.mean(o * o, axis=-1, keepdims=True)
            y = (o * lax.rsqrt(ms + EPS) * nw) * _silu(gg_ref[pl.ds(r, ch), cs].astype(F32))
            y_ref[pl.ds(r, ch), cs] = y.astype(BF16)
        return carry

    lax.fori_loop(0, RT // ch, epi, 0)
    for u in range(U):
        for d in range(2):
            for h in range(GLA_H):
                st_ref[u, d, h] = st_s[u, d, h].T


GLA_UNITS = {CTX_T: 4, LAT_T: 2}


def _gla(z, wlr, blr, s0_t, nw, *, T, n_units, unit0):
    U = GLA_UNITS[T]
    RT = U * T
    rb = lambda s: s + unit0 * T // RT
    hv = GLA_H * GLA_DV
    qw = GLA_H * GLA_DK
    st_spec = pl.BlockSpec((U, 2, GLA_H, GLA_DV, GLA_DK), lambda s: (s, 0, 0, 0, 0))
    in_specs = [pl.BlockSpec((RT, 2 * qw), lambda s: (rb(s), Z_GQ // (2 * qw))),
                pl.BlockSpec((RT, hv), lambda s: (rb(s), Z_GV // hv)),
                pl.BlockSpec((RT, hv), lambda s: (rb(s), Z_GG // hv)),
                pl.BlockSpec((RT, LANE), lambda s: (rb(s), Z_LR // LANE)),
                pl.BlockSpec((2, LANE, qw), lambda s: (0, 0, 0)),
                pl.BlockSpec((2, 1, qw), lambda s: (0, 0, 0)),
                st_spec,
                pl.BlockSpec((1, GLA_DV), lambda s: (0, 0))]
    args = [z, z, z, z, wlr, blr, s0_t, nw.reshape(1, GLA_DV)]
    return pl.pallas_call(
        functools.partial(_gla_body, T=T, U=U),
        grid=(n_units // U,),
        in_specs=in_specs,
        out_specs=[pl.BlockSpec((RT, hv), lambda s: (s, 0)),
                   pl.BlockSpec((U, 2, GLA_H, GLA_DK, GLA_DV), lambda s: (s, 0, 0, 0, 0))],
        out_shape=[jax.ShapeDtypeStruct((n_units * T, hv), BF16),
                   jax.ShapeDtypeStruct((n_units, 2, GLA_H, GLA_DK, GLA_DV), F32)],
        scratch_shapes=[pltpu.VMEM((2, RT, qw), F32),
                        pltpu.VMEM((2, RT, hv), F32),
                        pltpu.VMEM((U, 2, GLA_H, GLA_DV, GLA_DK), F32),
                        pltpu.VMEM((2, RT, qw), BF16),
                        pltpu.VMEM((2, RT, qw), BF16),
                        pltpu.VMEM((2, RT // GLA_C, qw), F32)],
        compiler_params=_cparams(("arbitrary",)),
        name=f"gla_T{T}",
    )(*args)


def _fft_body(x_ref, w2_ref, dt_ref, y_ref, p_s, *, T, U):
    xb = x_ref[...]
    for g in range(FFT_G):
        cs = slice(g * FFT_CH, (g + 1) * FFT_CH)
        p = _dot(xb[:, cs], w2_ref[...])
        for u in range(U):
            rows = slice(u * T, (u + 1) * T)
            p_s[u, 0:T, cs] = p[rows, :FFT_CH].astype(BF16)
            p_s[u, T:2 * T, cs] = p[rows, FFT_CH:].astype(BF16)
    for u in range(U):
        y_ref[u * T:(u + 1) * T, :] = _dot(dt_ref[...], p_s[u]).astype(BF16)


def _dft_consts(T):
    c = np.arange(FFT_CH)
    ang_c = (np.outer(c, c) % FFT_CH) * (2.0 * np.pi / FFT_CH)
    w2 = np.concatenate([np.cos(ang_c), np.sin(ang_c)], axis=1) / np.sqrt(FFT_CH)
    t = np.arange(T)
    ang_t = (np.outer(t, t) % T) * (2.0 * np.pi / T)
    dt = np.concatenate([np.cos(ang_t), -np.sin(ang_t)], axis=1) / np.sqrt(T)
    return jnp.asarray(w2, F32).astype(BF16), jnp.asarray(dt, F32).astype(BF16)


FFT_UNITS = {CTX_T: 4, LAT_T: 1}


def _fft(z, *, T, n_units, unit0):
    w2, dt = _dft_consts(T)
    fw = FFT_G * FFT_CH
    U = FFT_UNITS[T]
    RT = U * T
    in_specs = [pl.BlockSpec((RT, fw), lambda s: (s + unit0 * T // RT, Z_FF // fw)),
                pl.BlockSpec((FFT_CH, 2 * FFT_CH), lambda s: (0, 0)),
                pl.BlockSpec((T, 2 * T), lambda s: (0, 0))]
    args = [z, w2, dt]
    return pl.pallas_call(
        functools.partial(_fft_body, T=T, U=U),
        grid=(n_units // U,),
        in_specs=in_specs,
        out_specs=pl.BlockSpec((RT, fw), lambda s: (s, 0)),
        out_shape=jax.ShapeDtypeStruct((n_units * T, fw), BF16),
        scratch_shapes=[pltpu.VMEM((U, 2 * T, fw), BF16)],
        compiler_params=_cparams(("arbitrary",)),
        name=f"fft_T{T}",
    )(*args)


def _rope_tables(T):
    half = HEAD_DIM // 2
    inv = ROPE_BASE ** (-np.arange(0, half, 2, dtype=np.float64) / half)
    t = np.arange(T)
    ang_r = (t // GRID_W)[:, None] * inv[None, :]
    ang_c = (t % GRID_W)[:, None] * inv[None, :]
    cos = np.concatenate([np.cos(ang_r)] * 2 + [np.cos(ang_c)] * 2, axis=1)
    sin = np.concatenate([-np.sin(ang_r), np.sin(ang_r), -np.sin(ang_c), np.sin(ang_c)], axis=1)
    return jnp.asarray(cos, F32), jnp.asarray(sin, F32)


def _rope(x, cos, sin):
    lane = lax.broadcasted_iota(jnp.int32, x.shape, 1)
    quarter = HEAD_DIM // 4
    partner = jnp.where(lane % (2 * quarter) < quarter,
                        pltpu.roll(x, HEAD_DIM - quarter, 1), pltpu.roll(x, quarter, 1))
    return x * cos + partner * sin


def _rms128(x, w):
    return x * lax.rsqrt(jnp.mean(x * x, axis=-1, keepdims=True) + EPS) * w


def _attn_body(*refs, T, latent):
    if latent:
        (sink_ref, q_ref, k_ref, v_ref, qw_ref, kw_ref, ck_ref, cv_ref, cos_ref, sin_ref, y_ref) = refs
    else:
        (sink_ref, q_ref, k_ref, v_ref, qw_ref, kw_ref, y_ref, ko_ref, vo_ref) = refs
    kv = pl.program_id(1)
    nb = T // ATT_BLOCK
    kn = _rms128(k_ref[...].astype(F32), kw_ref[...])
    vb = v_ref[...]
    if latent:
        kn = _rope(kn, cos_ref[...], sin_ref[...])
        ckb = ck_ref[...].astype(BF16)
        cvb = cv_ref[...].astype(BF16)
    else:
        ko_ref[...] = kn
        vo_ref[...] = vb.astype(F32)
    kb = kn.astype(BF16)
    rows_g = lax.broadcasted_iota(jnp.int32, (GQA * ATT_BLOCK, 1), 0) // ATT_BLOCK
    sink = jnp.zeros((GQA * ATT_BLOCK, 1), F32)
    for g in range(GQA):
        sink = jnp.where(rows_g == g, sink_ref[kv * GQA + g], sink)
    scale = HEAD_DIM ** -0.5
    for n in range(nb):
        rs = slice(n * ATT_BLOCK, (n + 1) * ATT_BLOCK)
        qs = []
        for g in range(GQA):
            qn = _rms128(q_ref[rs, g * HEAD_DIM:(g + 1) * HEAD_DIM].astype(F32), qw_ref[...])
            if latent:
                qn = _rope(qn, cos_ref[rs, :], sin_ref[rs, :])
            qs.append((qn * scale).astype(BF16))
        qq = jnp.concatenate(qs, axis=0)
        if latent:
            lo, hi = max(n - 1, 0), min(n + 2, nb)
            ks = slice(lo * ATT_BLOCK, hi * ATT_BLOCK)
            w = (hi - lo) * ATT_BLOCK
            s_loc = _dot_nt(qq, kb[ks])
            qpos = n * ATT_BLOCK + lax.broadcasted_iota(jnp.int32, (GQA * ATT_BLOCK, w), 0) % ATT_BLOCK
            kpos = lo * ATT_BLOCK + lax.broadcasted_iota(jnp.int32, (GQA * ATT_BLOCK, w), 1)
            s_loc = jnp.where(jnp.abs(kpos - qpos) <= WINDOW, s_loc, NEG_INF)
            s_ctx = _dot_nt(qq, ckb)
            m = jnp.maximum(jnp.maximum(jnp.max(s_loc, axis=-1, keepdims=True),
                                        jnp.max(s_ctx, axis=-1, keepdims=True)), sink)
            p_loc = jnp.exp(s_loc - m)
            p_ctx = jnp.exp(s_ctx - m)
            den = (jnp.sum(p_loc, axis=-1, keepdims=True) + jnp.sum(p_ctx, axis=-1, keepdims=True)
                   + jnp.exp(sink - m))
            o = _dot(p_loc.astype(BF16), vb[ks]) + _dot(p_ctx.astype(BF16), cvb)
        else:
            s = _dot_nt(qq, kb)
            m = jnp.maximum(jnp.max(s, axis=-1, keepdims=True), sink)
            p = jnp.exp(s - m)
            den = jnp.sum(p, axis=-1, keepdims=True) + jnp.exp(sink - m)
            o = _dot(p.astype(BF16), vb)
        o = o * (1.0 / den)
        for g in range(GQA):
            y_ref[rs, g * HEAD_DIM:(g + 1) * HEAD_DIM] = o[g * ATT_BLOCK:(g + 1) * ATT_BLOCK].astype(BF16)


def _attn(z, sink_l, qw, kw, *, T, n_units, unit0, ctx_k=None, ctx_v=None):
    latent = ctx_k is not None
    qwid = GQA * HEAD_DIM
    rb = lambda u: u + unit0
    smem = pl.BlockSpec(memory_space=pltpu.SMEM)
    in_specs = [smem,
                pl.BlockSpec((T, qwid), lambda u, h: (rb(u), Z_AQ // qwid + h)),
                pl.BlockSpec((T, HEAD_DIM), lambda u, h: (rb(u), Z_AK // HEAD_DIM + h)),
                pl.BlockSpec((T, HEAD_DIM), lambda u, h: (rb(u), Z_AV // HEAD_DIM + h)),
                pl.BlockSpec((1, HEAD_DIM), lambda u, h: (0, 0)),
                pl.BlockSpec((1, HEAD_DIM), lambda u, h: (0, 0))]
    args = [sink_l, z, z, z, qw.reshape(1, HEAD_DIM), kw.reshape(1, HEAD_DIM)]
    y_spec = pl.BlockSpec((T, qwid), lambda u, h: (u, h))
    y_shape = jax.ShapeDtypeStruct((n_units * T, N_HEADS * HEAD_DIM), BF16)
    if latent:
        cos, sin = _rope_tables(T)
        in_specs += [pl.BlockSpec((None, PAST, HEAD_DIM), lambda u, h: (u, 0, h)),
                     pl.BlockSpec((None, PAST, HEAD_DIM), lambda u, h: (u, 0, h)),
                     pl.BlockSpec((T, HEAD_DIM), lambda u, h: (0, 0)),
                     pl.BlockSpec((T, HEAD_DIM), lambda u, h: (0, 0))]
        args += [ctx_k, ctx_v, cos, sin]
        out_specs, out_shape = y_spec, y_shape
    else:
        kv_spec = pl.BlockSpec((T, HEAD_DIM), lambda u, h: (u, h))
        kv_shape = jax.ShapeDtypeStruct((n_units * T, N_KV * HEAD_DIM), F32)
        out_specs, out_shape = [y_spec, kv_spec, kv_spec], [y_shape, kv_shape, kv_shape]
    return pl.pallas_call(
        functools.partial(_attn_body, T=T, latent=latent),
        grid=(n_units, N_KV),
        in_specs=in_specs,
        out_specs=out_specs,
        out_shape=out_shape,
        compiler_params=_cparams(("arbitrary", "arbitrary")),
        name=f"attn_T{T}",
    )(*args)


def _outproj_body(*refs, router, tm):
    y_refs, refs = refs[:6], refs[6:]
    if router:
        (w_ref, xa_ref, xb_ref, mod_ref, n2_ref, r_ref, x1_ref, h2_ref, idx_ref, gate_ref, cnt_ref, cnt_s) = refs
    else:
        (w_ref, xa_ref, xb_ref, mod_ref, n2_ref, x1_ref, h2_ref) = refs
    gw = GLA_H * GLA_DV
    fw = FFT_G * FFT_CH
    yg, yf, ya = (_pick_x(y_refs[2 * j], y_refs[2 * j + 1], tm) for j in range(3))
    acc = _dot(yg, w_ref[0:gw, :]) + _dot(yf, w_ref[gw:gw + fw, :]) + _dot(ya, w_ref[gw + fw:, :])
    x1 = _pick_x(xa_ref, xb_ref, tm) + mod_ref[2:3, :] * acc
    x1_ref[...] = x1
    h2 = _norm_mod(x1, n2_ref[...], mod_ref[4:5, :], mod_ref[3:4, :])
    h2_ref[...] = h2.astype(h2_ref.dtype)
    if router:
        logits = _dot(h2.astype(BF16), r_ref[...])
        lane = lax.broadcasted_iota(jnp.int32, logits.shape, 1)
        lg = jnp.where(lane < N_EXP, logits, -jnp.inf)
        v1 = jnp.max(lg, axis=-1, keepdims=True)
        i1 = jnp.min(jnp.where(lg == v1, lane, LANE), axis=-1, keepdims=True)
        lg2 = jnp.where(lane == i1, -jnp.inf, lg)
        v2 = jnp.max(lg2, axis=-1, keepdims=True)
        i2 = jnp.min(jnp.where(lg2 == v2, lane, LANE), axis=-1, keepdims=True)
        e = jnp.exp(v2 - v1)
        g1 = 1.0 / (1.0 + e)
        gate_ref[...] = jnp.where(lane == 0, g1, jnp.where(lane == 1, e * g1, 0.0))

        @pl.when(pl.program_id(0) == 0)
        def _():
            cnt_s[...] = jnp.zeros_like(cnt_s)

        oh1 = lane == i1
        oh2 = lane == i2
        oh = jnp.where(jnp.logical_or(oh1, oh2), 1.0, 0.0)
        ri = lax.broadcasted_iota(jnp.int32, (tm, tm), 0)
        ci = lax.broadcasted_iota(jnp.int32, (tm, tm), 1)
        before = jnp.where(ci < ri, 1.0, 0.0).astype(BF16)
        prior = _dot(before, oh.astype(BF16)) + cnt_s[0:1, :]
        r1 = jnp.sum(jnp.where(oh1, prior, 0.0), axis=-1, keepdims=True).astype(jnp.int32)
        r2 = jnp.sum(jnp.where(oh2, prior, 0.0), axis=-1, keepdims=True).astype(jnp.int32)
        idx_ref[...] = jnp.where(lane == 0, i1, jnp.where(lane == 1, i2,
                                 jnp.where(lane == 2, r1, jnp.where(lane == 3, r2, 0))))
        cnt_s[...] = cnt_s[...] + jnp.sum(oh, axis=0, keepdims=True)
        cnt_ref[...] = cnt_s[...]


def _outproj(y_gla, y_fft, y_att, w_out_b, x, mod, n2, router_b=None):
    tm = 512
    router = router_b is not None
    gw, fw, aw = GLA_H * GLA_DV, FFT_G * FFT_CH, N_HEADS * HEAD_DIM
    row = lambda i: (i, 0)
    xa, xb, off_b = _split_x(x, tm)
    in_specs, args = [], []
    for pair, wid in ((y_gla, gw), (y_fft, fw), (y_att, aw)):
        in_specs += _x_specs(tm, 0, wid)
        args += list(pair)
    in_specs.append(pl.BlockSpec((gw + fw + aw, D), lambda i: (0, 0), pipeline_mode=pl.Buffered(1)))
    in_specs += _x_specs(tm, off_b, D)
    in_specs += [pl.BlockSpec((None, 6, D), lambda i: (_mod_row(i, tm), 0, 0)),
                 pl.BlockSpec((1, D), lambda i: (0, 0))]
    args += [w_out_b, xa, xb, mod, n2.reshape(1, D)]
    out_specs = [pl.BlockSpec((tm, D), row), pl.BlockSpec((tm, D), row)]
    out_shape = [jax.ShapeDtypeStruct((N_TOK, D), F32),
                 jax.ShapeDtypeStruct((N_TOK, D), F32 if router else BF16)]
    if router:
        in_specs.append(pl.BlockSpec((D, LANE), lambda i: (0, 0)))
        args.append(router_b)
        out_specs += [pl.BlockSpec((tm, LANE), row), pl.BlockSpec((tm, LANE), row),
                      pl.BlockSpec((8, LANE), lambda i: (0, 0))]
        out_shape += [jax.ShapeDtypeStruct((N_TOK, LANE), jnp.int32), jax.ShapeDtypeStruct((N_TOK, LANE), F32),
                      jax.ShapeDtypeStruct((8, LANE), F32)]
    return pl.pallas_call(
        functools.partial(_outproj_body, router=router, tm=tm),
        grid=(N_TOK // tm,),
        in_specs=in_specs,
        out_specs=out_specs,
        out_shape=out_shape,
        scratch_shapes=[pltpu.VMEM((8, LANE), F32)] if router else [],
        compiler_params=_cparams(("arbitrary",)),
        name="outproj_router" if router else "outproj",
    )(*args)


FF_TF = 512
FF_KA = D_FF // FF_TF


def _store_f_slice(g_s, g, s, rows=slice(None)):
    for kk in range(FF_KA):
        @pl.when(s == kk)
        def _(kk=kk):
            g_s[rows, kk * FF_TF:(kk + 1) * FF_TF] = g


def _ffn_body(h_ref, w1_ref, w3_ref, w2_ref, x_ref, mod_ref, o_ref, g_s):
    s = pl.program_id(1)

    @pl.when(s < FF_KA)
    def _():
        h = h_ref[...]
        g = (_silu(_dot(h, w1_ref[...])) * _dot(h, w3_ref[...])).astype(BF16)
        _store_f_slice(g_s, g, s)

    @pl.when(s >= FF_KA)
    def _():
        o_ref[...] = x_ref[...] + mod_ref[5:6, :] * _dot(g_s[...], w2_ref[...])


def _ffn(h2, w1_b, w3_b, w2_b, x1, mod):
    tm, tn = 1024, 512
    kb = D // tn
    up = lambda i, s: (0, jnp.minimum(s, FF_KA - 1))
    down = lambda s: jnp.maximum(s - FF_KA, 0)
    return pl.pallas_call(
        _ffn_body,
        grid=(N_TOK // tm, FF_KA + kb),
        in_specs=[pl.BlockSpec((tm, D), lambda i, s: (i, 0)),
                  pl.BlockSpec((D, FF_TF), up),
                  pl.BlockSpec((D, FF_TF), up),
                  pl.BlockSpec((D_FF, tn), lambda i, s: (0, down(s))),
                  pl.BlockSpec((tm, tn), lambda i, s: (i, down(s))),
                  pl.BlockSpec((None, 6, tn), lambda i, s: (_mod_row(i, tm), 0, down(s)))],
        out_specs=pl.BlockSpec((tm, tn), lambda i, s: (i, down(s))),
        out_shape=jax.ShapeDtypeStruct((N_TOK, D), F32),
        scratch_shapes=[pltpu.VMEM((tm, D_FF), BF16)],
        compiler_params=_cparams(("arbitrary", "arbitrary")),
        name="ffn_dense",
    )(h2, w1_b, w3_b, w2_b, x1, mod)


MOE_TN = 256
MOE_KB = D // MOE_TN
MOE_GATHER = 256


def _moe_body(te_ref, tr_ref, src_ref, h_hbm, w1_ref, w3_ref, w2_ref, y_ref, xg_s, xb_s, g_s, sem):
    i = pl.program_id(0)
    s = pl.program_id(1)
    n_rows = tr_ref[i]
    parts = MOE_TM // MOE_GATHER
    cases = tuple((slice(0, q * MOE_GATHER),
                   jnp.logical_and(n_rows > (q - 1) * MOE_GATHER, n_rows <= q * MOE_GATHER))
                  for q in range(1, parts + 1))

    def row_copy(r, t):
        return pltpu.make_async_copy(h_hbm.at[pl.ds(t, 1)], xg_s.at[pl.ds(r, 1)], sem)

    for p in range(parts):
        @pl.when(jnp.logical_and(n_rows > p * MOE_GATHER, s == 0))
        def _(p=p):
            base = i * MOE_TM + p * MOE_GATHER

            def issue(r, c):
                row_copy(r, src_ref[base + r]).start()
                return c

            lax.fori_loop(0, MOE_GATHER, issue, 0, unroll=8)
            pltpu.make_async_copy(h_hbm.at[pl.ds(0, MOE_GATHER)], xg_s, sem).wait()
            xb_s[p * MOE_GATHER:(p + 1) * MOE_GATHER, :] = xg_s[...].astype(BF16)

    for rows, cond in cases:
        @pl.when(jnp.logical_and(cond, s < FF_KA))
        def _(rows=rows):
            h = xb_s[rows, :]
            g = (_silu(_dot(h, w1_ref[...].astype(BF16))) * _dot(h, w3_ref[...].astype(BF16))).astype(BF16)
            _store_f_slice(g_s, g, s, rows)

        @pl.when(jnp.logical_and(cond, s >= FF_KA))
        def _(rows=rows):
            y_ref[rows, :] = _dot(g_s[rows, :], w2_ref[...].astype(BF16))

    for p in range(parts):
        @pl.when(n_rows <= p * MOE_GATHER)
        def _(p=p):
            y_ref[p * MOE_GATHER:(p + 1) * MOE_GATHER, :] = jnp.zeros((MOE_GATHER, MOE_TN), F32)


def _moe(tile_exp, tile_rows, row_src, h2, w1, w3, w2):
    def up(i, s, te, tr, src):
        return (te[i], 0, jnp.where(tr[i] > 0, jnp.minimum(s, FF_KA - 1), FF_KA - 1))

    def down(i, s, te, tr, src):
        return (te[i], 0, jnp.where(tr[i] > 0, jnp.maximum(s - FF_KA, 0), MOE_KB - 1))

    return pl.pallas_call(
        _moe_body,
        grid_spec=pltpu.PrefetchScalarGridSpec(
            num_scalar_prefetch=3,
            grid=(MOE_TILES, FF_KA + MOE_KB),
            in_specs=[pl.BlockSpec(memory_space=pl.ANY),
                      pl.BlockSpec((None, D, FF_TF), up),
                      pl.BlockSpec((None, D, FF_TF), up),
                      pl.BlockSpec((None, D_FF, MOE_TN), down)],
            out_specs=pl.BlockSpec((MOE_TM, MOE_TN), lambda i, s, te, tr, src: (i, jnp.maximum(s - FF_KA, 0))),
            scratch_shapes=[pltpu.VMEM((MOE_GATHER, D), F32), pltpu.VMEM((MOE_TM, D), BF16),
                            pltpu.VMEM((MOE_TM, D_FF), BF16), pltpu.SemaphoreType.DMA(())],
        ),
        out_shape=jax.ShapeDtypeStruct((MOE_ROWS, D), F32),
        compiler_params=_cparams(("arbitrary", "arbitrary")),
        name="moe_experts",
    )(tile_exp, tile_rows, row_src, h2, w1, w3, w2)


def _combine_body(slot_ref, y_hbm, x_ref, g_ref, mod_ref, oc_ref, ol_ref, buf, sems, *, tc):
    i = pl.program_id(0)

    def gather(tile, b):
        base = tile * tc

        def issue(r, c):
            for k in range(2):
                pltpu.make_async_copy(y_hbm.at[pl.ds(slot_ref[2 * (base + r) + k], 1)],
                                      buf.at[b, k, pl.ds(r, 1)], sems.at[b]).start()
            return c

        lax.fori_loop(0, tc, issue, 0, unroll=8)

    def wait(b):
        for k in range(2):
            pltpu.make_async_copy(y_hbm.at[pl.ds(0, tc)], buf.at[b, k], sems.at[b]).wait()

    @pl.when(i == 0)
    def _():
        gather(0, 0)

    for b in range(2):
        @pl.when(jnp.logical_and(i % 2 == b, i + 1 < pl.num_programs(0)))
        def _(b=b):
            gather(i + 1, 1 - b)

    for b in range(2):
        @pl.when(i % 2 == b)
        def _(b=b):
            wait(b)
            g = g_ref[...]
            f = g[:, 0:1] * buf[b, 0] + g[:, 1:2] * buf[b, 1]
            _combine_store(x_ref[...] + mod_ref[5:6, :] * f, i, oc_ref, ol_ref, tc)


def _combine_store(out, i, oc_ref, ol_ref, tc):
    is_ctx = i < N_CTX // tc

    @pl.when(is_ctx)
    def _():
        oc_ref[...] = out

    @pl.when(jnp.logical_not(is_ctx))
    def _():
        ol_ref[...] = out


def _combine(slot, y, x1, gates, mod):
    tc = 256
    n_ctx_tiles = N_CTX // tc
    return pl.pallas_call(
        functools.partial(_combine_body, tc=tc),
        grid_spec=pltpu.PrefetchScalarGridSpec(
            num_scalar_prefetch=1,
            grid=(N_TOK // tc,),
            in_specs=[pl.BlockSpec(memory_space=pl.ANY),
                      pl.BlockSpec((tc, D), lambda i, s: (i, 0)),
                      pl.BlockSpec((tc, LANE), lambda i, s: (i, 0)),
                      pl.BlockSpec((None, 6, D), lambda i, s: (_mod_row(i, tc), 0, 0))],
            out_specs=[pl.BlockSpec((tc, D), lambda i, s: (jnp.minimum(i, n_ctx_tiles - 1), 0)),
                       pl.BlockSpec((tc, D), lambda i, s: (jnp.maximum(i - n_ctx_tiles, 0), 0))],
            scratch_shapes=[pltpu.VMEM((2, 2, tc, D), F32), pltpu.SemaphoreType.DMA((2,))],
        ),
        out_shape=[jax.ShapeDtypeStruct((N_CTX, D), F32), jax.ShapeDtypeStruct((N_LAT, D), F32)],
        compiler_params=_cparams(("arbitrary",)),
        name="moe_combine",
    )(slot, y, x1, gates, mod)


def _route_meta(idx4, counts):
    counts = counts.astype(jnp.int32)
    tiles = (counts + MOE_TM - 1) // MOE_TM
    tile_end = jnp.cumsum(tiles)
    tile_start = tile_end - tiles
    n_tiles = tile_end[-1]
    exp_of = idx4[:, 0:2].reshape(-1)
    rank = idx4[:, 2:4].reshape(-1)
    first_row = jnp.sum(jnp.where(exp_of[:, None] == jnp.arange(N_EXP, dtype=jnp.int32)[None, :],
                                  (tile_start * MOE_TM)[None, :], 0), axis=1)
    slot = (first_row + rank).astype(jnp.int32)
    j = jnp.arange(MOE_TILES, dtype=jnp.int32)
    te = jnp.sum((j[:, None] >= tile_end[None, :]).astype(jnp.int32), axis=1)
    te = jnp.where(j < n_tiles, te, te[jnp.maximum(n_tiles - 1, 0)])
    te = jnp.minimum(te, N_EXP - 1).astype(jnp.int32)
    rows = jnp.clip(counts[te] - (j - tile_start[te]) * MOE_TM, 0, MOE_TM)
    rows = jnp.where(j < n_tiles, rows, 0).astype(jnp.int32)
    tok = jnp.arange(2 * N_TOK, dtype=jnp.int32) // 2
    row_src = jnp.zeros((MOE_ROWS,), jnp.int32).at[slot].set(tok, unique_indices=True)
    return te, rows, row_src, slot


IN_W = 3616
LR_AT = Z_FF


def _prep_w_in_body(w_ref, o_ref):
    lr_w = 2 * GLA_RANK
    rows = w_ref.shape[0]
    o_ref[:, 0:LR_AT] = w_ref[:, 0:LR_AT].astype(BF16)
    o_ref[:, LR_AT:Z_LR] = w_ref[:, LR_AT + lr_w:IN_W].astype(BF16)
    tail = jnp.concatenate([w_ref[:, LR_AT:LR_AT + lr_w], jnp.zeros((rows, LANE - lr_w), F32)], axis=1)
    o_ref[:, Z_LR:Z_LR + LANE] = tail.astype(BF16)
    o_ref[:, Z_LR + LANE:] = jnp.zeros((rows, Z_W - Z_LR - LANE), BF16)


def _prep_w_in(w_all, l):
    tr = 256
    return pl.pallas_call(
        _prep_w_in_body,
        grid=(D // tr,),
        in_specs=[pl.BlockSpec((None, tr, IN_W), lambda i: (l, i, 0))],
        out_specs=pl.BlockSpec((tr, Z_W), lambda i: (i, 0)),
        out_shape=jax.ShapeDtypeStruct((D, Z_W), BF16),
        compiler_params=_cparams(("arbitrary",)),
        name="prep_w_in",
    )(w_all)


def _prep_w_lr(w_lr2):
    out = jnp.zeros((2, LANE, GLA_H * GLA_DK), F32)
    out = out.at[0, 0:GLA_RANK].set(w_lr2[0]).at[1, GLA_RANK:2 * GLA_RANK].set(w_lr2[1])
    return out.astype(BF16)


def kernel(x_prompt, x_sample, cache_k, cache_v, state_gla, c, c_ctx, w_ada, b_ada, norm1_w, norm2_w, w_in, w_gla_lr2, b_gla_lr2, gla_norm_w, q_norm_w, k_norm_w, attn_sink, w_out, ffn_w1, ffn_w3, ffn_w2, moe_router, moe_w1, moe_w3, moe_w2):
    x = (x_prompt.reshape(N_CTX, D), x_sample.reshape(N_LAT, D))
    cond16 = jnp.concatenate([c_ctx[None, :], c, jnp.zeros((16 - 1 - N_LAT_B, D), F32)], axis=0)
    mods = _mods(cond16, w_ada, b_ada).reshape(DEPTH, 16, 6, D)
    ck_all = cache_k.reshape(N_LAT_B, DEPTH, PAST, N_KV * HEAD_DIM)
    cv_all = cache_v.reshape(N_LAT_B, DEPTH, PAST, N_KV * HEAD_DIM)
    s0_all = jnp.swapaxes(state_gla, -1, -2)
    zero_state = jnp.zeros((N_CTX_B, 2, GLA_H, GLA_DV, GLA_DK), F32)
    ctx_units = dict(T=CTX_T, n_units=N_CTX_B, unit0=0)
    lat_units = dict(T=LAT_T, n_units=N_LAT_B, unit0=N_CTX // LAT_T)

    ks, vs, ss = [], [], []
    for l in range(DEPTH):
        mod = mods[l]
        z = _inproj(x, mod, norm1_w[l], _prep_w_in(w_in, l))
        wlr = _prep_w_lr(w_gla_lr2[l])
        blr = b_gla_lr2[l].reshape(2, 1, GLA_H * GLA_DK)
        yg_c, st_c = _gla(z, wlr, blr, zero_state, gla_norm_w[l], **ctx_units)
        yg_l, _ = _gla(z, wlr, blr, s0_all[:, l], gla_norm_w[l], **lat_units)
        y_gla = (yg_c, yg_l)
        y_fft = (_fft(z, **ctx_units), _fft(z, **lat_units))
        ya_c, k_c, v_c = _attn(z, attn_sink[l], q_norm_w[l], k_norm_w[l], **ctx_units)
        ya_l = _attn(z, attn_sink[l], q_norm_w[l], k_norm_w[l], ctx_k=ck_all[:, l], ctx_v=cv_all[:, l],
                     **lat_units)
        y_att = (ya_c, ya_l)
        w_out_b = w_out[l].astype(BF16)
        e = l // 2
        if l % 2 == 0:
            x1, h2 = _outproj(y_gla, y_fft, y_att, w_out_b, x, mod, norm2_w[l])
            x = _ffn(h2, ffn_w1[e].astype(BF16), ffn_w3[e].astype(BF16), ffn_w2[e].astype(BF16), x1, mod)
        else:
            router_b = jnp.pad(moe_router[e], ((0, 0), (0, LANE - N_EXP))).astype(BF16)
            x1, h2, idx, gates, counts = _outproj(y_gla, y_fft, y_att, w_out_b, x, mod, norm2_w[l], router_b)
            te, tile_rows, row_src, slot = _route_meta(idx[:, :4], counts[0, :N_EXP])
            y = _moe(te, tile_rows, row_src, h2, moe_w1[e], moe_w3[e], moe_w2[e])
            x = _combine(slot, y, x1, gates, mod)
        ks.append(k_c.reshape(N_CTX_B, CTX_T, N_KV, HEAD_DIM))
        vs.append(v_c.reshape(N_CTX_B, CTX_T, N_KV, HEAD_DIM))
        ss.append(st_c)
    if not isinstance(x, (tuple, list)):
        x = (x[:N_CTX], x[N_CTX:])
    y_prompt = x[0].reshape(N_CTX_B, CTX_T, D)
    y_sample = x[1].reshape(N_LAT_B, LAT_T, D)
    return (y_prompt, y_sample, jnp.stack(ks, axis=1), jnp.stack(vs, axis=1), jnp.stack(ss, axis=1))
```

```python
import functools

import numpy as np
import jax
import jax.numpy as jnp
from jax import lax
from jax.experimental import pallas as pl
from jax.experimental.pallas import tpu as pltpu

F32 = jnp.float32
BF16 = jnp.bfloat16

D = 2048
N_CTX_B, CTX_T = 32, 256
N_LAT_B, LAT_T = 8, 1024
N_CTX = N_CTX_B * CTX_T
N_LAT = N_LAT_B * LAT_T
N_TOK = N_CTX + N_LAT
DEPTH = 2
PAST = 512
GRID_W = 64
GLA_H, GLA_DK, GLA_DV, GLA_RANK, GLA_TAU, GLA_C = 4, 64, 128, 16, 16.0, 32
FFT_G, FFT_CH = 4, 128
N_HEADS, N_KV, HEAD_DIM = 8, 2, 128
GQA = N_HEADS // N_KV
WINDOW, ATT_BLOCK = 128, 128
ROPE_BASE = 10000.0
D_FF = 5632
N_EXP = 8
EPS = 1e-6
NEG_INF = -1e30

Z_GQ, Z_GK, Z_GV, Z_GG, Z_FF, Z_AQ, Z_AK, Z_AV, Z_LR = 0, 256, 512, 1024, 1536, 2048, 3072, 3328, 3584
Z_W = 3840
LANE = 128

VMEM_LIMIT = 56 * 1024 * 1024

MOE_TM = 1024
MOE_TILES = 2 * N_TOK // MOE_TM + N_EXP
MOE_ROWS = MOE_TILES * MOE_TM


def _cparams(sem):
    return pltpu.CompilerParams(dimension_semantics=sem, vmem_limit_bytes=VMEM_LIMIT)


def _silu(x):
    return x / (1.0 + jnp.exp(-x))


def _norm_mod(x, nw, scale, shift):
    ms = jnp.mean(x * x, axis=-1, keepdims=True)
    return (x * lax.rsqrt(ms + EPS) * nw) * (1.0 + scale) + shift


def _dot(a, b):
    return jnp.dot(a, b, preferred_element_type=F32)


def _dot_nt(a, b):
    return lax.dot_general(a, b, (((1,), (1,)), ((), ())), preferred_element_type=F32)


def _dot_tn(a, b):
    return lax.dot_general(a, b, (((0,), (0,)), ((), ())), preferred_element_type=F32)


def _mod_body(c_ref, w_ref, b_ref, o_ref):
    s = _silu(c_ref[...])
    o_ref[...] = _dot(s.astype(BF16), w_ref[...].astype(BF16)) + b_ref[...]


def _mods(cond16, w_ada, b_ada):
    tn = 1024
    return pl.pallas_call(
        _mod_body,
        grid=(DEPTH, 6 * D // tn),
        in_specs=[pl.BlockSpec((16, D), lambda l, j: (0, 0)),
                  pl.BlockSpec((None, D, tn), lambda l, j: (l, 0, j)),
                  pl.BlockSpec((None, 1, tn), lambda l, j: (l, 0, j))],
        out_specs=pl.BlockSpec((None, 16, tn), lambda l, j: (l, 0, j)),
        out_shape=jax.ShapeDtypeStruct((DEPTH, 16, 6 * D), F32),
        compiler_params=_cparams(("arbitrary", "arbitrary")),
        name="adaln_mod",
    )(cond16, w_ada, b_ada.reshape(DEPTH, 1, 6 * D))


def _mod_row(i, tm):
    per_lat = LAT_T // tm
    n_ctx_tiles = N_CTX // tm
    return jnp.where(i < n_ctx_tiles, 0, 1 + (i - n_ctx_tiles) // per_lat)


def _split_x(x, tm):
    if isinstance(x, (tuple, list)):
        return x[0], x[1], 0
    return x, x, N_CTX // tm


def _x_specs(tm, off_b, width):
    n_ctx_tiles = N_CTX // tm
    return [pl.BlockSpec((tm, width), lambda i: (jnp.minimum(i, n_ctx_tiles - 1), 0)),
            pl.BlockSpec((tm, width), lambda i: (jnp.maximum(i - n_ctx_tiles, 0) + off_b, 0))]


def _pick_x(xa_ref, xb_ref, tm):
    return jnp.where(pl.program_id(0) < N_CTX // tm, xa_ref[...], xb_ref[...])


def _inproj_body(xa_ref, xb_ref, mod_ref, n_ref, w_ref, o_ref, *, tm):
    h = _norm_mod(_pick_x(xa_ref, xb_ref, tm), n_ref[...], mod_ref[1:2, :], mod_ref[0:1, :])
    o_ref[...] = _dot(h.astype(BF16), w_ref[...]).astype(BF16)


def _inproj(x, mod, n1, w_in_b):
    tm = 512
    xa, xb, off_b = _split_x(x, tm)
    return pl.pallas_call(
        functools.partial(_inproj_body, tm=tm),
        grid=(N_TOK // tm,),
        in_specs=_x_specs(tm, off_b, D) + [
            pl.BlockSpec((None, 6, D), lambda i: (_mod_row(i, tm), 0, 0)),
            pl.BlockSpec((1, D), lambda i: (0, 0)),
            pl.BlockSpec((D, Z_W), lambda i: (0, 0), pipeline_mode=pl.Buffered(1))],
        out_specs=pl.BlockSpec((tm, Z_W), lambda i: (i, 0)),
        out_shape=jax.ShapeDtypeStruct((N_TOK, Z_W), BF16),
        compiler_params=_cparams(("arbitrary",)),
        name="inproj",
    )(xa, xb, mod, n1.reshape(1, D), w_in_b)


def _gla_body(qk_ref, v_ref, gg_ref, lr_ref, wlr_ref, blr_ref, s0_ref, nw_ref, y_ref, st_ref,
              g_s, o_s, st_s, qt_s, ke_s, dec_s, *, T, U):
    C = GLA_C
    NC = T // C
    RT = U * T
    qw = GLA_H * GLA_DK
    lr = lr_ref[...]
    for d in range(2):
        x = _dot(lr, wlr_ref[d]) + blr_ref[d]
        g_s[d] = (jnp.minimum(x, 0.0) - jnp.log(1.0 + jnp.exp(-jnp.abs(x)))) * (1.0 / GLA_TAU)
    st_s[...] = s0_ref[...]

    R = 128
    cpb = R // C
    ri = lax.broadcasted_iota(jnp.int32, (R, R), 0)
    ci = lax.broadcasted_iota(jnp.int32, (R, R), 1)
    same = (ri // C) == (ci // C)
    masks = (jnp.logical_and(same, ci <= ri), jnp.logical_and(same, ci >= ri))
    sum_ops = tuple(jnp.concatenate([m.astype(BF16), same.astype(BF16)], axis=0) for m in masks)

    def intra(i, carry):
        r0 = pl.multiple_of(i * R, R)
        q = qk_ref[pl.ds(r0, R), 0:qw].astype(F32) * (GLA_DK ** -0.5)
        k = qk_ref[pl.ds(r0, R), qw:2 * qw].astype(F32)
        vb = v_ref[pl.ds(r0, R), :]
        heads = [(d, h) for d in range(2) for h in range(GLA_H)]
        ks = [slice(h * GLA_DK, (h + 1) * GLA_DK) for h in range(GLA_H)]
        sums = []
        for d in range(2):
            g = g_s[d, pl.ds(r0, R), :]
            g_hi = g.astype(BF16)
            r1 = g - g_hi.astype(F32)
            g_mid = r1.astype(BF16)
            g_lo = (r1 - g_mid.astype(F32)).astype(BF16)
            sums.append(_dot(sum_ops[d], jnp.concatenate([g_hi, g_mid, g_lo], axis=1)))
        qts, kts = [], []
        for d in range(2):
            s3 = sums[d][:, 0:qw] + sums[d][:, qw:2 * qw] + sums[d][:, 2 * qw:3 * qw]
            b = s3[0:R]
            bl = s3[R:2 * R]
            qts.append((q * jnp.exp(b)).astype(BF16))
            kts.append((k * jnp.exp(-b)).astype(BF16))
            qt_s[d, pl.ds(r0, R), :] = qts[d]
            ke_s[d, pl.ds(r0, R), :] = (k * jnp.exp(bl - b)).astype(BF16)
            ebl = jnp.exp(bl)
            for j in range(cpb):
                dec_s[d, pl.ds(i * cpb + j, 1), :] = ebl[j * C:j * C + 1, :]
        a = [_dot_nt(qts[d][:, ks[h]], kts[d][:, ks[h]]) for d, h in heads]
        a = [jnp.where(masks[d], a[n], 0.0).astype(BF16) for n, (d, h) in enumerate(heads)]
        o = [_dot(a[n], vb[:, h * GLA_DV:(h + 1) * GLA_DV]) for n, (d, h) in enumerate(heads)]
        for d in range(2):
            o_s[d, pl.ds(r0, R), :] = jnp.concatenate(o[d * GLA_H:(d + 1) * GLA_H], axis=1)
        return carry

    lax.fori_loop(0, RT // R, intra, 0)

    def inter(i, carry):
        ks = [slice(h * GLA_DK, (h + 1) * GLA_DK) for h in range(GLA_H)]
        for u in range(U):
            for d in range(2):
                c = u * NC + (i if d == 0 else NC - 1 - i)
                r0 = pl.multiple_of(c * C, C)
                qt = qt_s[d, pl.ds(r0, C), :]
                ke = ke_s[d, pl.ds(r0, C), :]
                vb = v_ref[pl.ds(r0, C), :]
                dec = dec_s[d, pl.ds(c, 1), :]
                o = []
                for h in range(GLA_H):
                    s_t = st_s[u, d, h]
                    o.append(_dot_nt(qt[:, ks[h]], s_t.astype(BF16)))
                    u_t = _dot_tn(vb[:, h * GLA_DV:(h + 1) * GLA_DV], ke[:, ks[h]])
                    st_s[u, d, h] = s_t * dec[:, ks[h]] + u_t
                o_s[d, pl.ds(r0, C), :] += jnp.concatenate(o, axis=1)
        return carry

    lax.fori_loop(0, NC, inter, 0)

    ch = 128
    nw = nw_ref[...]

    def epi(i, carry):
        r = pl.multiple_of(i * ch, ch)
        for h in range(GLA_H):
            cs = slice(h * GLA_DV, (h + 1) * GLA_DV)
            o = o_s[0, pl.ds(r, ch), cs] + o_s[1, pl.ds(r, ch), cs]
            ms = jnp.mean(o * o, axis=-1, keepdims=True)
            y = (o * lax.rsqrt(ms + EPS) * nw) * _silu(gg_ref[pl.ds(r, ch), cs].astype(F32))
            y_ref[pl.ds(r, ch), cs] = y.astype(BF16)
        return carry

    lax.fori_loop(0, RT // ch, epi, 0)
    for u in range(U):
        for d in range(2):
            for h in range(GLA_H):
                st_ref[u, d, h] = st_s[u, d, h].T


GLA_UNITS = {CTX_T: 4, LAT_T: 2}


def _gla(z, wlr, blr, s0_t, nw, *, T, n_units, unit0):
    U = GLA_UNITS[T]
    RT = U * T
    rb = lambda s: s + unit0 * T // RT
    hv = GLA_H * GLA_DV
    qw = GLA_H * GLA_DK
    st_spec = pl.BlockSpec((U, 2, GLA_H, GLA_DV, GLA_DK), lambda s: (s, 0, 0, 0, 0))
    in_specs = [pl.BlockSpec((RT, 2 * qw), lambda s: (rb(s), Z_GQ // (2 * qw))),
                pl.BlockSpec((RT, hv), lambda s: (rb(s), Z_GV // hv)),
                pl.BlockSpec((RT, hv), lambda s: (rb(s), Z_GG // hv)),
                pl.BlockSpec((RT, LANE), lambda s: (rb(s), Z_LR // LANE)),
                pl.BlockSpec((2, LANE, qw), lambda s: (0, 0, 0)),
                pl.BlockSpec((2, 1, qw), lambda s: (0, 0, 0)),
                st_spec,
                pl.BlockSpec((1, GLA_DV), lambda s: (0, 0))]
    args = [z, z, z, z, wlr, blr, s0_t, nw.reshape(1, GLA_DV)]
    return pl.pallas_call(
        functools.partial(_gla_body, T=T, U=U),
        grid=(n_units // U,),
        in_specs=in_specs,
        out_specs=[pl.BlockSpec((RT, hv), lambda s: (s, 0)),
                   pl.BlockSpec((U, 2, GLA_H, GLA_DK, GLA_DV), lambda s: (s, 0, 0, 0, 0))],
        out_shape=[jax.ShapeDtypeStruct((n_units * T, hv), BF16),
                   jax.ShapeDtypeStruct((n_units, 2, GLA_H, GLA_DK, GLA_DV), F32)],
        scratch_shapes=[pltpu.VMEM((2, RT, qw), F32),
                        pltpu.VMEM((2, RT, hv), F32),
                        pltpu.VMEM((U, 2, GLA_H, GLA_DV, GLA_DK), F32),
                        pltpu.VMEM((2, RT, qw), BF16),
                        pltpu.VMEM((2, RT, qw), BF16),
                        pltpu.VMEM((2, RT // GLA_C, qw), F32)],
        compiler_params=_cparams(("arbitrary",)),
        name=f"gla_T{T}",
    )(*args)


def _fft_body(x_ref, w2_ref, dt_ref, y_ref, p_s, *, T, U):
    xb = x_ref[...]
    for g in range(FFT_G):
        cs = slice(g * FFT_CH, (g + 1) * FFT_CH)
        p = _dot(xb[:, cs], w2_ref[...])
        for u in range(U):
            rows = slice(u * T, (u + 1) * T)
            p_s[u, 0:T, cs] = p[rows, :FFT_CH].astype(BF16)
            p_s[u, T:2 * T, cs] = p[rows, FFT_CH:].astype(BF16)
    for u in range(U):
        y_ref[u * T:(u + 1) * T, :] = _dot(dt_ref[...], p_s[u]).astype(BF16)


def _dft_consts(T):
    c = np.arange(FFT_CH)
    ang_c = (np.outer(c, c) % FFT_CH) * (2.0 * np.pi / FFT_CH)
    w2 = np.concatenate([np.cos(ang_c), np.sin(ang_c)], axis=1) / np.sqrt(FFT_CH)
    t = np.arange(T)
    ang_t = (np.outer(t, t) % T) * (2.0 * np.pi / T)
    dt = np.concatenate([np.cos(ang_t), -np.sin(ang_t)], axis=1) / np.sqrt(T)
    return jnp.asarray(w2, F32).astype(BF16), jnp.asarray(dt, F32).astype(BF16)


FFT_UNITS = {CTX_T: 4, LAT_T: 1}


def _fft(z, *, T, n_units, unit0):
    w2, dt = _dft_consts(T)
    fw = FFT_G * FFT_CH
    U = FFT_UNITS[T]
    RT = U * T
    in_specs = [pl.BlockSpec((RT, fw), lambda s: (s + unit0 * T // RT, Z_FF // fw)),
                pl.BlockSpec((FFT_CH, 2 * FFT_CH), lambda s: (0, 0)),
                pl.BlockSpec((T, 2 * T), lambda s: (0, 0))]
    args = [z, w2, dt]
    return pl.pallas_call(
        functools.partial(_fft_body, T=T, U=U),
        grid=(n_units // U,),
        in_specs=in_specs,
        out_specs=pl.BlockSpec((RT, fw), lambda s: (s, 0)),
        out_shape=jax.ShapeDtypeStruct((n_units * T, fw), BF16),
        scratch_shapes=[pltpu.VMEM((U, 2 * T, fw), BF16)],
        compiler_params=_cparams(("arbitrary",)),
        name=f"fft_T{T}",
    )(*args)


def _rope_tables(T):
    half = HEAD_DIM // 2
    inv = ROPE_BASE ** (-np.arange(0, half, 2, dtype=np.float64) / half)
    t = np.arange(T)
    ang_r = (t // GRID_W)[:, None] * inv[None, :]
    ang_c = (t % GRID_W)[:, None] * inv[None, :]
    cos = np.concatenate([np.cos(ang_r)] * 2 + [np.cos(ang_c)] * 2, axis=1)
    sin = np.concatenate([-np.sin(ang_r), np.sin(ang_r), -np.sin(ang_c), np.sin(ang_c)], axis=1)
    return jnp.asarray(cos, F32), jnp.asarray(sin, F32)


def _rope(x, cos, sin):
    lane = lax.broadcasted_iota(jnp.int32, x.shape, 1)
    quarter = HEAD_DIM // 4
    partner = jnp.where(lane % (2 * quarter) < quarter,
                        pltpu.roll(x, HEAD_DIM - quarter, 1), pltpu.roll(x, quarter, 1))
    return x * cos + partner * sin


def _rms128(x, w):
    return x * lax.rsqrt(jnp.mean(x * x, axis=-1, keepdims=True) + EPS) * w


def _attn_body(*refs, T, U, latent):
    assert U == 1 or not latent
    if latent:
        (sink_ref, q_ref, k_ref, v_ref, qw_ref, kw_ref, ck_ref, cv_ref, cos_ref, sin_ref, y_ref) = refs
    else:
        (sink_ref, q_ref, k_ref, v_ref, qw_ref, kw_ref, y_ref, ko_ref, vo_ref) = refs
    kv = pl.program_id(1)
    nb = T // ATT_BLOCK
    kn = _rms128(k_ref[...].astype(F32), kw_ref[...])
    vb = v_ref[...]
    if latent:
        kn = _rope(kn, cos_ref[...], sin_ref[...])
        ckb = ck_ref[...].astype(BF16)
        cvb = cv_ref[...].astype(BF16)
    else:
        ko_ref[...] = kn
        vo_ref[...] = vb.astype(F32)
    kb = kn.astype(BF16)
    rows_g = lax.broadcasted_iota(jnp.int32, (GQA * ATT_BLOCK, 1), 0) // ATT_BLOCK
    sink = jnp.zeros((GQA * ATT_BLOCK, 1), F32)
    for g in range(GQA):
        sink = jnp.where(rows_g == g, sink_ref[kv * GQA + g], sink)
    scale = HEAD_DIM ** -0.5
    for u, n in [(u, n) for u in range(U) for n in range(nb)]:
        seq = slice(u * T, (u + 1) * T)
        rs = slice(u * T + n * ATT_BLOCK, u * T + (n + 1) * ATT_BLOCK)
        qs = []
        for g in range(GQA):
            qn = _rms128(q_ref[rs, g * HEAD_DIM:(g + 1) * HEAD_DIM].astype(F32), qw_ref[...])
            if latent:
                qn = _rope(qn, cos_ref[rs, :], sin_ref[rs, :])
            qs.append((qn * scale).astype(BF16))
        qq = jnp.concatenate(qs, axis=0)
        if latent:
            lo, hi = max(n - 1, 0), min(n + 2, nb)
            ks = slice(lo * ATT_BLOCK, hi * ATT_BLOCK)
            w = (hi - lo) * ATT_BLOCK
            s_loc = _dot_nt(qq, kb[ks])
            qpos = n * ATT_BLOCK + lax.broadcasted_iota(jnp.int32, (GQA * ATT_BLOCK, w), 0) % ATT_BLOCK
            kpos = lo * ATT_BLOCK + lax.broadcasted_iota(jnp.int32, (GQA * ATT_BLOCK, w), 1)
            s_loc = jnp.where(jnp.abs(kpos - qpos) <= WINDOW, s_loc, NEG_INF)
            s_ctx = _dot_nt(qq, ckb)
            m = jnp.maximum(jnp.maximum(jnp.max(s_loc, axis=-1, keepdims=True),
                                        jnp.max(s_ctx, axis=-1, keepdims=True)), sink)
            p_loc = jnp.exp(s_loc - m)
            p_ctx = jnp.exp(s_ctx - m)
            den = (jnp.sum(p_loc, axis=-1, keepdims=True) + jnp.sum(p_ctx, axis=-1, keepdims=True)
                   + jnp.exp(sink - m))
            o = _dot(p_loc.astype(BF16), vb[ks]) + _dot(p_ctx.astype(BF16), cvb)
        else:
            s = _dot_nt(qq, kb[seq])
            m = jnp.maximum(jnp.max(s, axis=-1, keepdims=True), sink)
            p = jnp.exp(s - m)
            den = jnp.sum(p, axis=-1, keepdims=True) + jnp.exp(sink - m)
            o = _dot(p.astype(BF16), vb[seq])
        o = o * (1.0 / den)
        for g in range(GQA):
            y_ref[rs, g * HEAD_DIM:(g + 1) * HEAD_DIM] = o[g * ATT_BLOCK:(g + 1) * ATT_BLOCK].astype(BF16)


ATT_UNITS = {CTX_T: 4, LAT_T: 1}


def _attn(z, sink_l, qw, kw, *, T, n_units, unit0, ctx_k=None, ctx_v=None):
    latent = ctx_k is not None
    qwid = GQA * HEAD_DIM
    U = ATT_UNITS[T]
    RT = U * T
    rb = lambda u: u + unit0 * T // RT
    smem = pl.BlockSpec(memory_space=pltpu.SMEM)
    in_specs = [smem,
                pl.BlockSpec((RT, qwid), lambda u, h: (rb(u), Z_AQ // qwid + h)),
                pl.BlockSpec((RT, HEAD_DIM), lambda u, h: (rb(u), Z_AK // HEAD_DIM + h)),
                pl.BlockSpec((RT, HEAD_DIM), lambda u, h: (rb(u), Z_AV // HEAD_DIM + h)),
                pl.BlockSpec((1, HEAD_DIM), lambda u, h: (0, 0)),
                pl.BlockSpec((1, HEAD_DIM), lambda u, h: (0, 0))]
    args = [sink_l, z, z, z, qw.reshape(1, HEAD_DIM), kw.reshape(1, HEAD_DIM)]
    y_spec = pl.BlockSpec((RT, qwid), lambda u, h: (u, h))
    y_shape = jax.ShapeDtypeStruct((n_units * T, N_HEADS * HEAD_DIM), BF16)
    if latent:
        cos, sin = _rope_tables(T)
        in_specs += [pl.BlockSpec((None, PAST, HEAD_DIM), lambda u, h: (u, 0, h)),
                     pl.BlockSpec((None, PAST, HEAD_DIM), lambda u, h: (u, 0, h)),
                     pl.BlockSpec((T, HEAD_DIM), lambda u, h: (0, 0)),
                     pl.BlockSpec((T, HEAD_DIM), lambda u, h: (0, 0))]
        args += [ctx_k, ctx_v, cos, sin]
        out_specs, out_shape = y_spec, y_shape
    else:
        kv_spec = pl.BlockSpec((RT, HEAD_DIM), lambda u, h: (u, h))
        kv_shape = jax.ShapeDtypeStruct((n_units * T, N_KV * HEAD_DIM), F32)
        out_specs, out_shape = [y_spec, kv_spec, kv_spec], [y_shape, kv_shape, kv_shape]
    return pl.pallas_call(
        functools.partial(_attn_body, T=T, U=U, latent=latent),
        grid=(n_units // U, N_KV),
        in_specs=in_specs,
        out_specs=out_specs,
        out_shape=out_shape,
        compiler_params=_cparams(("arbitrary", "arbitrary")),
        name=f"attn_T{T}",
    )(*args)


def _outproj_body(*refs, router, tm):
    y_refs, refs = refs[:6], refs[6:]
    if router:
        (w_ref, xa_ref, xb_ref, mod_ref, n2_ref, r_ref, x1_ref, h2_ref, idx_ref, gate_ref, cnt_ref, cnt_s) = refs
    else:
        (w_ref, xa_ref, xb_ref, mod_ref, n2_ref, x1_ref, h2_ref) = refs
    gw = GLA_H * GLA_DV
    fw = FFT_G * FFT_CH
    yg, yf, ya = (_pick_x(y_refs[2 * j], y_refs[2 * j + 1], tm) for j in range(3))
    acc = _dot(yg, w_ref[0:gw, :]) + _dot(yf, w_ref[gw:gw + fw, :]) + _dot(ya, w_ref[gw + fw:, :])
    x1 = _pick_x(xa_ref, xb_ref, tm) + mod_ref[2:3, :] * acc
    x1_ref[...] = x1
    h2 = _norm_mod(x1, n2_ref[...], mod_ref[4:5, :], mod_ref[3:4, :])
    h2_ref[...] = h2.astype(h2_ref.dtype)
    if router:
        logits = _dot(h2.astype(BF16), r_ref[...])
        lane = lax.broadcasted_iota(jnp.int32, logits.shape, 1)
        lg = jnp.where(lane < N_EXP, logits, -jnp.inf)
        v1 = jnp.max(lg, axis=-1, keepdims=True)
        i1 = jnp.min(jnp.where(lg == v1, lane, LANE), axis=-1, keepdims=True)
        lg2 = jnp.where(lane == i1, -jnp.inf, lg)
        v2 = jnp.max(lg2, axis=-1, keepdims=True)
        i2 = jnp.min(jnp.where(lg2 == v2, lane, LANE), axis=-1, keepdims=True)
        e = jnp.exp(v2 - v1)
        g1 = 1.0 / (1.0 + e)
        gate_ref[...] = jnp.where(lane == 0, g1, jnp.where(lane == 1, e * g1, 0.0))

        @pl.when(pl.program_id(0) == 0)
        def _():
            cnt_s[...] = jnp.zeros_like(cnt_s)

        oh1 = lane == i1
        oh2 = lane == i2
        oh = jnp.where(jnp.logical_or(oh1, oh2), 1.0, 0.0)
        ri = lax.broadcasted_iota(jnp.int32, (tm, tm), 0)
        ci = lax.broadcasted_iota(jnp.int32, (tm, tm), 1)
        before = jnp.where(ci < ri, 1.0, 0.0).astype(BF16)
        prior = _dot(before, oh.astype(BF16)) + cnt_s[0:1, :]
        r1 = jnp.sum(jnp.where(oh1, prior, 0.0), axis=-1, keepdims=True).astype(jnp.int32)
        r2 = jnp.sum(jnp.where(oh2, prior, 0.0), axis=-1, keepdims=True).astype(jnp.int32)
        idx_ref[...] = jnp.where(lane == 0, i1, jnp.where(lane == 1, i2,
                                 jnp.where(lane == 2, r1, jnp.where(lane == 3, r2, 0))))
        cnt_s[...] = cnt_s[...] + jnp.sum(oh, axis=0, keepdims=True)
        cnt_ref[...] = cnt_s[...]


def _outproj(y_gla, y_fft, y_att, w_out_b, x, mod, n2, router_b=None):
    tm = 512
    router = router_b is not None
    gw, fw, aw = GLA_H * GLA_DV, FFT_G * FFT_CH, N_HEADS * HEAD_DIM
    row = lambda i: (i, 0)
    xa, xb, off_b = _split_x(x, tm)
    in_specs, args = [], []
    for pair, wid in ((y_gla, gw), (y_fft, fw), (y_att, aw)):
        in_specs += _x_specs(tm, 0, wid)
        args += list(pair)
    in_specs.append(pl.BlockSpec((gw + fw + aw, D), lambda i: (0, 0), pipeline_mode=pl.Buffered(1)))
    in_specs += _x_specs(tm, off_b, D)
    in_specs += [pl.BlockSpec((None, 6, D), lambda i: (_mod_row(i, tm), 0, 0)),
                 pl.BlockSpec((1, D), lambda i: (0, 0))]
    args += [w_out_b, xa, xb, mod, n2.reshape(1, D)]
    out_specs = [pl.BlockSpec((tm, D), row), pl.BlockSpec((tm, D), row)]
    out_shape = [jax.ShapeDtypeStruct((N_TOK, D), F32),
                 jax.ShapeDtypeStruct((N_TOK, D), F32 if router else BF16)]
    if router:
        in_specs.append(pl.BlockSpec((D, LANE), lambda i: (0, 0)))
        args.append(router_b)
        out_specs += [pl.BlockSpec((tm, LANE), row), pl.BlockSpec((tm, LANE), row),
                      pl.BlockSpec((8, LANE), lambda i: (0, 0))]
        out_shape += [jax.ShapeDtypeStruct((N_TOK, LANE), jnp.int32), jax.ShapeDtypeStruct((N_TOK, LANE), F32),
                      jax.ShapeDtypeStruct((8, LANE), F32)]
    return pl.pallas_call(
        functools.partial(_outproj_body, router=router, tm=tm),
        grid=(N_TOK // tm,),
        in_specs=in_specs,
        out_specs=out_specs,
        out_shape=out_shape,
        scratch_shapes=[pltpu.VMEM((8, LANE), F32)] if router else [],
        compiler_params=_cparams(("arbitrary",)),
        name="outproj_router" if router else "outproj",
    )(*args)


FF_TF = 512
FF_KA = D_FF // FF_TF


def _store_f_slice(g_s, g, s, rows=slice(None)):
    for kk in range(FF_KA):
        @pl.when(s == kk)
        def _(kk=kk):
            g_s[rows, kk * FF_TF:(kk + 1) * FF_TF] = g


def _ffn_body(h_ref, w1_ref, w3_ref, w2_ref, x_ref, mod_ref, o_ref, g_s):
    s = pl.program_id(1)

    @pl.when(s < FF_KA)
    def _():
        h = h_ref[...]
        g = (_silu(_dot(h, w1_ref[...])) * _dot(h, w3_ref[...])).astype(BF16)
        _store_f_slice(g_s, g, s)

    @pl.when(s >= FF_KA)
    def _():
        o_ref[...] = x_ref[...] + mod_ref[5:6, :] * _dot(g_s[...], w2_ref[...])


def _ffn(h2, w1_b, w3_b, w2_b, x1, mod):
    tm, tn = 1024, 512
    kb = D // tn
    up = lambda i, s: (0, jnp.minimum(s, FF_KA - 1))
    down = lambda s: jnp.maximum(s - FF_KA, 0)
    return pl.pallas_call(
        _ffn_body,
        grid=(N_TOK // tm, FF_KA + kb),
        in_specs=[pl.BlockSpec((tm, D), lambda i, s: (i, 0)),
                  pl.BlockSpec((D, FF_TF), up),
                  pl.BlockSpec((D, FF_TF), up),
                  pl.BlockSpec((D_FF, tn), lambda i, s: (0, down(s))),
                  pl.BlockSpec((tm, tn), lambda i, s: (i, down(s))),
                  pl.BlockSpec((None, 6, tn), lambda i, s: (_mod_row(i, tm), 0, down(s)))],
        out_specs=pl.BlockSpec((tm, tn), lambda i, s: (i, down(s))),
        out_shape=jax.ShapeDtypeStruct((N_TOK, D), F32),
        scratch_shapes=[pltpu.VMEM((tm, D_FF), BF16)],
        compiler_params=_cparams(("arbitrary", "arbitrary")),
        name="ffn_dense",
    )(h2, w1_b, w3_b, w2_b, x1, mod)


MOE_TN = 256
MOE_KB = D // MOE_TN
MOE_GATHER = 256


def _moe_body(te_ref, tr_ref, src_ref, h_hbm, w1_ref, w3_ref, w2_ref, y_ref, xg_s, xb_s, g_s, sem):
    i = pl.program_id(0)
    s = pl.program_id(1)
    n_rows = tr_ref[i]
    parts = MOE_TM // MOE_GATHER
    cases = tuple((slice(0, q * MOE_GATHER),
                   jnp.logical_and(n_rows > (q - 1) * MOE_GATHER, n_rows <= q * MOE_GATHER))
                  for q in range(1, parts + 1))

    def row_copy(r, t):
        return pltpu.make_async_copy(h_hbm.at[pl.ds(t, 1)], xg_s.at[pl.ds(r, 1)], sem)

    for p in range(parts):
        @pl.when(jnp.logical_and(n_rows > p * MOE_GATHER, s == 0))
        def _(p=p):
            base = i * MOE_TM + p * MOE_GATHER

            def issue(r, c):
                row_copy(r, src_ref[base + r]).start()
                return c

            lax.fori_loop(0, MOE_GATHER, issue, 0, unroll=8)
            pltpu.make_async_copy(h_hbm.at[pl.ds(0, MOE_GATHER)], xg_s, sem).wait()
            xb_s[p * MOE_GATHER:(p + 1) * MOE_GATHER, :] = xg_s[...].astype(BF16)

    for rows, cond in cases:
        @pl.when(jnp.logical_and(cond, s < FF_KA))
        def _(rows=rows):
            h = xb_s[rows, :]
            g = (_silu(_dot(h, w1_ref[...].astype(BF16))) * _dot(h, w3_ref[...].astype(BF16))).astype(BF16)
            _store_f_slice(g_s, g, s, rows)

        @pl.when(jnp.logical_and(cond, s >= FF_KA))
        def _(rows=rows):
            y_ref[rows, :] = _dot(g_s[rows, :], w2_ref[...].astype(BF16))

    for p in range(parts):
        @pl.when(n_rows <= p * MOE_GATHER)
        def _(p=p):
            y_ref[p * MOE_GATHER:(p + 1) * MOE_GATHER, :] = jnp.zeros((MOE_GATHER, MOE_TN), F32)


def _moe(tile_exp, tile_rows, row_src, h2, w1, w3, w2):
    def up(i, s, te, tr, src):
        return (te[i], 0, jnp.where(tr[i] > 0, jnp.minimum(s, FF_KA - 1), FF_KA - 1))

    def down(i, s, te, tr, src):
        return (te[i], 0, jnp.where(tr[i] > 0, jnp.maximum(s - FF_KA, 0), MOE_KB - 1))

    return pl.pallas_call(
        _moe_body,
        grid_spec=pltpu.PrefetchScalarGridSpec(
            num_scalar_prefetch=3,
            grid=(MOE_TILES, FF_KA + MOE_KB),
            in_specs=[pl.BlockSpec(memory_space=pl.ANY),
                      pl.BlockSpec((None, D, FF_TF), up),
                      pl.BlockSpec((None, D, FF_TF), up),
                      pl.BlockSpec((None, D_FF, MOE_TN), down)],
            out_specs=pl.BlockSpec((MOE_TM, MOE_TN), lambda i, s, te, tr, src: (i, jnp.maximum(s - FF_KA, 0))),
            scratch_shapes=[pltpu.VMEM((MOE_GATHER, D), F32), pltpu.VMEM((MOE_TM, D), BF16),
                            pltpu.VMEM((MOE_TM, D_FF), BF16), pltpu.SemaphoreType.DMA(())],
        ),
        out_shape=jax.ShapeDtypeStruct((MOE_ROWS, D), F32),
        compiler_params=_cparams(("arbitrary", "arbitrary")),
        name="moe_experts",
    )(tile_exp, tile_rows, row_src, h2, w1, w3, w2)


def _combine_body(slot_ref, y_hbm, x_ref, g_ref, mod_ref, oc_ref, ol_ref, buf, sems, *, tc):
    i = pl.program_id(0)

    def gather(tile, b):
        base = tile * tc

        def issue(r, c):
            for k in range(2):
                pltpu.make_async_copy(y_hbm.at[pl.ds(slot_ref[2 * (base + r) + k], 1)],
                                      buf.at[b, k, pl.ds(r, 1)], sems.at[b]).start()
            return c

        lax.fori_loop(0, tc, issue, 0, unroll=8)

    def wait(b):
        for k in range(2):
            pltpu.make_async_copy(y_hbm.at[pl.ds(0, tc)], buf.at[b, k], sems.at[b]).wait()

    @pl.when(i == 0)
    def _():
        gather(0, 0)

    for b in range(2):
        @pl.when(jnp.logical_and(i % 2 == b, i + 1 < pl.num_programs(0)))
        def _(b=b):
            gather(i + 1, 1 - b)

    for b in range(2):
        @pl.when(i % 2 == b)
        def _(b=b):
            wait(b)
            g = g_ref[...]
            f = g[:, 0:1] * buf[b, 0] + g[:, 1:2] * buf[b, 1]
            _combine_store(x_ref[...] + mod_ref[5:6, :] * f, i, oc_ref, ol_ref, tc)


def _combine_store(out, i, oc_ref, ol_ref, tc):
    is_ctx = i < N_CTX // tc

    @pl.when(is_ctx)
    def _():
        oc_ref[...] = out

    @pl.when(jnp.logical_not(is_ctx))
    def _():
        ol_ref[...] = out


def _combine(slot, y, x1, gates, mod):
    tc = 256
    n_ctx_tiles = N_CTX // tc
    return pl.pallas_call(
        functools.partial(_combine_body, tc=tc),
        grid_spec=pltpu.PrefetchScalarGridSpec(
            num_scalar_prefetch=1,
            grid=(N_TOK // tc,),
            in_specs=[pl.BlockSpec(memory_space=pl.ANY),
                      pl.BlockSpec((tc, D), lambda i, s: (i, 0)),
                      pl.BlockSpec((tc, LANE), lambda i, s: (i, 0)),
                      pl.BlockSpec((None, 6, D), lambda i, s: (_mod_row(i, tc), 0, 0))],
            out_specs=[pl.BlockSpec((tc, D), lambda i, s: (jnp.minimum(i, n_ctx_tiles - 1), 0)),
                       pl.BlockSpec((tc, D), lambda i, s: (jnp.maximum(i - n_ctx_tiles, 0), 0))],
            scratch_shapes=[pltpu.VMEM((2, 2, tc, D), F32), pltpu.SemaphoreType.DMA((2,))],
        ),
        out_shape=[jax.ShapeDtypeStruct((N_CTX, D), F32), jax.ShapeDtypeStruct((N_LAT, D), F32)],
        compiler_params=_cparams(("arbitrary",)),
        name="moe_combine",
    )(slot, y, x1, gates, mod)


def _route_meta(idx4, counts):
    counts = counts.astype(jnp.int32)
    tiles = (counts + MOE_TM - 1) // MOE_TM
    tile_end = jnp.cumsum(tiles)
    tile_start = tile_end - tiles
    n_tiles = tile_end[-1]
    exp_of = idx4[:, 0:2].reshape(-1)
    rank = idx4[:, 2:4].reshape(-1)
    first_row = jnp.sum(jnp.where(exp_of[:, None] == jnp.arange(N_EXP, dtype=jnp.int32)[None, :],
                                  (tile_start * MOE_TM)[None, :], 0), axis=1)
    slot = (first_row + rank).astype(jnp.int32)
    j = jnp.arange(MOE_TILES, dtype=jnp.int32)
    te = jnp.sum((j[:, None] >= tile_end[None, :]).astype(jnp.int32), axis=1)
    te = jnp.where(j < n_tiles, te, te[jnp.maximum(n_tiles - 1, 0)])
    te = jnp.minimum(te, N_EXP - 1).astype(jnp.int32)
    rows = jnp.clip(counts[te] - (j - tile_start[te]) * MOE_TM, 0, MOE_TM)
    rows = jnp.where(j < n_tiles, rows, 0).astype(jnp.int32)
    tok = jnp.arange(2 * N_TOK, dtype=jnp.int32) // 2
    row_src = jnp.zeros((MOE_ROWS,), jnp.int32).at[slot].set(tok, unique_indices=True)
    return te, rows, row_src, slot


IN_W = 3616
LR_AT = Z_FF


def _prep_w_in_body(w_ref, o_ref):
    lr_w = 2 * GLA_RANK
    rows = w_ref.shape[0]
    o_ref[:, 0:LR_AT] = w_ref[:, 0:LR_AT].astype(BF16)
    o_ref[:, LR_AT:Z_LR] = w_ref[:, LR_AT + lr_w:IN_W].astype(BF16)
    tail = jnp.concatenate([w_ref[:, LR_AT:LR_AT + lr_w], jnp.zeros((rows, LANE - lr_w), F32)], axis=1)
    o_ref[:, Z_LR:Z_LR + LANE] = tail.astype(BF16)
    o_ref[:, Z_LR + LANE:] = jnp.zeros((rows, Z_W - Z_LR - LANE), BF16)


def _prep_w_in(w_all, l):
    tr = 256
    return pl.pallas_call(
        _prep_w_in_body,
        grid=(D // tr,),
        in_specs=[pl.BlockSpec((None, tr, IN_W), lambda i: (l, i, 0))],
        out_specs=pl.BlockSpec((tr, Z_W), lambda i: (i, 0)),
        out_shape=jax.ShapeDtypeStruct((D, Z_W), BF16),
        compiler_params=_cparams(("arbitrary",)),
        name="prep_w_in",
    )(w_all)


def _prep_w_lr(w_lr2):
    out = jnp.zeros((2, LANE, GLA_H * GLA_DK), F32)
    out = out.at[0, 0:GLA_RANK].set(w_lr2[0]).at[1, GLA_RANK:2 * GLA_RANK].set(w_lr2[1])
    return out.astype(BF16)


def kernel(x_prompt, x_sample, cache_k, cache_v, state_gla, c, c_ctx, w_ada, b_ada, norm1_w, norm2_w, w_in, w_gla_lr2, b_gla_lr2, gla_norm_w, q_norm_w, k_norm_w, attn_sink, w_out, ffn_w1, ffn_w3, ffn_w2, moe_router, moe_w1, moe_w3, moe_w2):
    x = (x_prompt.reshape(N_CTX, D), x_sample.reshape(N_LAT, D))
    cond16 = jnp.concatenate([c_ctx[None, :], c, jnp.zeros((16 - 1 - N_LAT_B, D), F32)], axis=0)
    mods = _mods(cond16, w_ada, b_ada).reshape(DEPTH, 16, 6, D)
    ck_all = cache_k.reshape(N_LAT_B, DEPTH, PAST, N_KV * HEAD_DIM)
    cv_all = cache_v.reshape(N_LAT_B, DEPTH, PAST, N_KV * HEAD_DIM)
    s0_all = jnp.swapaxes(state_gla, -1, -2)
    zero_state = jnp.zeros((N_CTX_B, 2, GLA_H, GLA_DV, GLA_DK), F32)
    ctx_units = dict(T=CTX_T, n_units=N_CTX_B, unit0=0)
    lat_units = dict(T=LAT_T, n_units=N_LAT_B, unit0=N_CTX // LAT_T)

    ks, vs, ss = [], [], []
    for l in range(DEPTH):
        mod = mods[l]
        z = _inproj(x, mod, norm1_w[l], _prep_w_in(w_in, l))
        wlr = _prep_w_lr(w_gla_lr2[l])
        blr = b_gla_lr2[l].reshape(2, 1, GLA_H * GLA_DK)
        yg_c, st_c = _gla(z, wlr, blr, zero_state, gla_norm_w[l], **ctx_units)
        yg_l, _ = _gla(z, wlr, blr, s0_all[:, l], gla_norm_w[l], **lat_units)
        y_gla = (yg_c, yg_l)
        y_fft = (_fft(z, **ctx_units), _fft(z, **lat_units))
        ya_c, k_c, v_c = _attn(z, attn_sink[l], q_norm_w[l], k_norm_w[l], **ctx_units)
        ya_l = _attn(z, attn_sink[l], q_norm_w[l], k_norm_w[l], ctx_k=ck_all[:, l], ctx_v=cv_all[:, l],
                     **lat_units)
        y_att = (ya_c, ya_l)
        w_out_b = w_out[l].astype(BF16)
        e = l // 2
        if l % 2 == 0:
            x1, h2 = _outproj(y_gla, y_fft, y_att, w_out_b, x, mod, norm2_w[l])
            x = _ffn(h2, ffn_w1[e].astype(BF16), ffn_w3[e].astype(BF16), ffn_w2[e].astype(BF16), x1, mod)
        else:
            router_b = jnp.pad(moe_router[e], ((0, 0), (0, LANE - N_EXP))).astype(BF16)
            x1, h2, idx, gates, counts = _outproj(y_gla, y_fft, y_att, w_out_b, x, mod, norm2_w[l], router_b)
            te, tile_rows, row_src, slot = _route_meta(idx[:, :4], counts[0, :N_EXP])
            y = _moe(te, tile_rows, row_src, h2, moe_w1[e], moe_w3[e], moe_w2[e])
            x = _combine(slot, y, x1, gates, mod)
        ks.append(k_c.reshape(N_CTX_B, CTX_T, N_KV, HEAD_DIM))
        vs.append(v_c.reshape(N_CTX_B, CTX_T, N_KV, HEAD_DIM))
        ss.append(st_c)
    if not isinstance(x, (tuple, list)):
        x = (x[:N_CTX], x[N_CTX:])
    y_prompt = x[0].reshape(N_CTX_B, CTX_T, D)
    y_sample = x[1].reshape(N_LAT_B, LAT_T, D)
    return (y_prompt, y_sample, jnp.stack(ks, axis=1), jnp.stack(vs, axis=1), jnp.stack(ss, axis=1))
```

```python
import functools

import numpy as np
import jax
import jax.numpy as jnp
from jax import lax
from jax.experimental import pallas as pl
from jax.experimental.pallas import tpu as pltpu

F32 = jnp.float32
BF16 = jnp.bfloat16

D = 2048
N_CTX_B, CTX_T = 32, 256
N_LAT_B, LAT_T = 8, 1024
N_CTX = N_CTX_B * CTX_T
N_LAT = N_LAT_B * LAT_T
N_TOK = N_CTX + N_LAT
DEPTH = 2
PAST = 512
GRID_W = 64
GLA_H, GLA_DK, GLA_DV, GLA_RANK, GLA_TAU, GLA_C = 4, 64, 128, 16, 16.0, 32
FFT_G, FFT_CH = 4, 128
N_HEADS, N_KV, HEAD_DIM = 8, 2, 128
GQA = N_HEADS // N_KV
WINDOW, ATT_BLOCK = 128, 128
ROPE_BASE = 10000.0
D_FF = 5632
N_EXP = 8
EPS = 1e-6
NEG_INF = -1e30

Z_GQ, Z_GK, Z_GV, Z_GG, Z_FF, Z_AQ, Z_AK, Z_AV, Z_LR = 0, 256, 512, 1024, 1536, 2048, 3072, 3328, 3584
Z_W = 3840
LANE = 128

VMEM_LIMIT = 56 * 1024 * 1024

MOE_TM = 1024
MOE_TILES = 2 * N_TOK // MOE_TM + N_EXP
MOE_ROWS = MOE_TILES * MOE_TM


def _cparams(sem):
    return pltpu.CompilerParams(dimension_semantics=sem, vmem_limit_bytes=VMEM_LIMIT)


def _silu(x):
    return x / (1.0 + jnp.exp(-x))


def _norm_mod(x, nw, scale, shift):
    ms = jnp.mean(x * x, axis=-1, keepdims=True)
    return (x * lax.rsqrt(ms + EPS) * nw) * (1.0 + scale) + shift


def _dot(a, b):
    return jnp.dot(a, b, preferred_element_type=F32)


def _dot_nt(a, b):
    return lax.dot_general(a, b, (((1,), (1,)), ((), ())), preferred_element_type=F32)


def _dot_tn(a, b):
    return lax.dot_general(a, b, (((0,), (0,)), ((), ())), preferred_element_type=F32)


def _mod_body(c_ref, w_ref, b_ref, o_ref):
    s = _silu(c_ref[...])
    o_ref[...] = _dot(s.astype(BF16), w_ref[...].astype(BF16)) + b_ref[...]


def _mods(cond16, w_ada, b_ada):
    tn = 1024
    return pl.pallas_call(
        _mod_body,
        grid=(DEPTH, 6 * D // tn),
        in_specs=[pl.BlockSpec((16, D), lambda l, j: (0, 0)),
                  pl.BlockSpec((None, D, tn), lambda l, j: (l, 0, j)),
                  pl.BlockSpec((None, 1, tn), lambda l, j: (l, 0, j))],
        out_specs=pl.BlockSpec((None, 16, tn), lambda l, j: (l, 0, j)),
        out_shape=jax.ShapeDtypeStruct((DEPTH, 16, 6 * D), F32),
        compiler_params=_cparams(("arbitrary", "arbitrary")),
        name="adaln_mod",
    )(cond16, w_ada, b_ada.reshape(DEPTH, 1, 6 * D))


def _mod_row(i, tm):
    per_lat = LAT_T // tm
    n_ctx_tiles = N_CTX // tm
    return jnp.where(i < n_ctx_tiles, 0, 1 + (i - n_ctx_tiles) // per_lat)


def _split_x(x, tm):
    if isinstance(x, (tuple, list)):
        return x[0], x[1], 0
    return x, x, N_CTX // tm


def _x_specs(tm, off_b, width):
    n_ctx_tiles = N_CTX // tm
    return [pl.BlockSpec((tm, width), lambda i: (jnp.minimum(i, n_ctx_tiles - 1), 0)),
            pl.BlockSpec((tm, width), lambda i: (jnp.maximum(i - n_ctx_tiles, 0) + off_b, 0))]


def _pick_x(xa_ref, xb_ref, tm):
    return jnp.where(pl.program_id(0) < N_CTX // tm, xa_ref[...], xb_ref[...])


def _inproj_body(xa_ref, xb_ref, mod_ref, n_ref, w_ref, o_ref, *, tm):
    h = _norm_mod(_pick_x(xa_ref, xb_ref, tm), n_ref[...], mod_ref[1:2, :], mod_ref[0:1, :])
    o_ref[...] = _dot(h.astype(BF16), w_ref[...]).astype(BF16)


def _inproj(x, mod, n1, w_in_b):
    tm = 512
    xa, xb, off_b = _split_x(x, tm)
    return pl.pallas_call(
        functools.partial(_inproj_body, tm=tm),
        grid=(N_TOK // tm,),
        in_specs=_x_specs(tm, off_b, D) + [
            pl.BlockSpec((None, 6, D), lambda i: (_mod_row(i, tm), 0, 0)),
            pl.BlockSpec((1, D), lambda i: (0, 0)),
            pl.BlockSpec((D, Z_W), lambda i: (0, 0), pipeline_mode=pl.Buffered(1))],
        out_specs=pl.BlockSpec((tm, Z_W), lambda i: (i, 0)),
        out_shape=jax.ShapeDtypeStruct((N_TOK, Z_W), BF16),
        compiler_params=_cparams(("arbitrary",)),
        name="inproj",
    )(xa, xb, mod, n1.reshape(1, D), w_in_b)


def _gla_body(qk_ref, v_ref, gg_ref, lr_ref, wlr_ref, blr_ref, s0_ref, nw_ref, y_ref, st_ref,
              g_s, o_s, st_s, qt_s, ke_s, dec_s, *, T, U):
    C = GLA_C
    NC = T // C
    RT = U * T
    qw = GLA_H * GLA_DK
    lr = lr_ref[...]
    for d in range(2):
        x = _dot(lr, wlr_ref[d]) + blr_ref[d]
        g_s[d] = (jnp.minimum(x, 0.0) - jnp.log(1.0 + jnp.exp(-jnp.abs(x)))) * (1.0 / GLA_TAU)
    st_s[...] = s0_ref[...]

    R = 128
    cpb = R // C
    ri = lax.broadcasted_iota(jnp.int32, (R, R), 0)
    ci = lax.broadcasted_iota(jnp.int32, (R, R), 1)
    same = (ri // C) == (ci // C)
    masks = (jnp.logical_and(same, ci <= ri), jnp.logical_and(same, ci >= ri))
    sum_ops = tuple(jnp.concatenate([m.astype(BF16), same.astype(BF16)], axis=0) for m in masks)

    def intra(i, carry):
        r0 = pl.multiple_of(i * R, R)
        q = qk_ref[pl.ds(r0, R), 0:qw].astype(F32) * (GLA_DK ** -0.5)
        k = qk_ref[pl.ds(r0, R), qw:2 * qw].astype(F32)
        vb = v_ref[pl.ds(r0, R), :]
        heads = [(d, h) for d in range(2) for h in range(GLA_H)]
        ks = [slice(h * GLA_DK, (h + 1) * GLA_DK) for h in range(GLA_H)]
        sums = []
        for d in range(2):
            g = g_s[d, pl.ds(r0, R), :]
            g_hi = g.astype(BF16)
            r1 = g - g_hi.astype(F32)
            g_mid = r1.astype(BF16)
            g_lo = (r1 - g_mid.astype(F32)).astype(BF16)
            sums.append(_dot(sum_ops[d], jnp.concatenate([g_hi, g_mid, g_lo], axis=1)))
        qts, kts = [], []
        for d in range(2):
            s3 = sums[d][:, 0:qw] + sums[d][:, qw:2 * qw] + sums[d][:, 2 * qw:3 * qw]
            b = s3[0:R]
            bl = s3[R:2 * R]
            qts.append((q * jnp.exp(b)).astype(BF16))
            kts.append((k * jnp.exp(-b)).astype(BF16))
            qt_s[d, pl.ds(r0, R), :] = qts[d]
            ke_s[d, pl.ds(r0, R), :] = (k * jnp.exp(bl - b)).astype(BF16)
            ebl = jnp.exp(bl)
            for j in range(cpb):
                dec_s[d, pl.ds(i * cpb + j, 1), :] = ebl[j * C:j * C + 1, :]
        a = [_dot_nt(qts[d][:, ks[h]], kts[d][:, ks[h]]) for d, h in heads]
        a = [jnp.where(masks[d], a[n], 0.0).astype(BF16) for n, (d, h) in enumerate(heads)]
        o = [_dot(a[n], vb[:, h * GLA_DV:(h + 1) * GLA_DV]) for n, (d, h) in enumerate(heads)]
        for d in range(2):
            o_s[d, pl.ds(r0, R), :] = jnp.concatenate(o[d * GLA_H:(d + 1) * GLA_H], axis=1)
        return carry

    lax.fori_loop(0, RT // R, intra, 0)

    def inter(i, carry):
        ks = [slice(h * GLA_DK, (h + 1) * GLA_DK) for h in range(GLA_H)]
        for u in range(U):
            for d in range(2):
                c = u * NC + (i if d == 0 else NC - 1 - i)
                r0 = pl.multiple_of(c * C, C)
                qt = qt_s[d, pl.ds(r0, C), :]
                ke = ke_s[d, pl.ds(r0, C), :]
                vb = v_ref[pl.ds(r0, C), :]
                dec = dec_s[d, pl.ds(c, 1), :]
                o = []
                for h in range(GLA_H):
                    s_t = st_s[u, d, h]
                    o.append(_dot_nt(qt[:, ks[h]], s_t.astype(BF16)))
                    u_t = _dot_tn(vb[:, h * GLA_DV:(h + 1) * GLA_DV], ke[:, ks[h]])
                    st_s[u, d, h] = s_t * dec[:, ks[h]] + u_t
                o_s[d, pl.ds(r0, C), :] += jnp.concatenate(o, axis=1)
        return carry

    lax.fori_loop(0, NC, inter, 0)

    ch = 128
    nw = nw_ref[...]

    def epi(i, carry):
        r = pl.multiple_of(i * ch, ch)
        for h in range(GLA_H):
            cs = slice(h * GLA_DV, (h + 1) * GLA_DV)
            o = o_s[0, pl.ds(r, ch), cs] + o_s[1, pl.ds(r, ch), cs]
            ms = jnp.mean(o * o, axis=-1, keepdims=True)
            y = (o * lax.rsqrt(ms + EPS) * nw) * _silu(gg_ref[pl.ds(r, ch), cs].astype(F32))
            y_ref[pl.ds(r, ch), cs] = y.astype(BF16)
        return carry

    lax.fori_loop(0, RT // ch, epi, 0)
    for u in range(U):
        for d in range(2):
            for h in range(GLA_H):
                st_ref[u, d, h] = st_s[u, d, h].T


GLA_UNITS = {CTX_T: 4, LAT_T: 2}


def _gla(z, wlr, blr, s0_t, nw, *, T, n_units, unit0):
    U = GLA_UNITS[T]
    RT = U * T
    rb = lambda s: s + unit0 * T // RT
    hv = GLA_H * GLA_DV
    qw = GLA_H * GLA_DK
    st_spec = pl.BlockSpec((U, 2, GLA_H, GLA_DV, GLA_DK), lambda s: (s, 0, 0, 0, 0))
    in_specs = [pl.BlockSpec((RT, 2 * qw), lambda s: (rb(s), Z_GQ // (2 * qw))),
                pl.BlockSpec((RT, hv), lambda s: (rb(s), Z_GV // hv)),
                pl.BlockSpec((RT, hv), lambda s: (rb(s), Z_GG // hv)),
                pl.BlockSpec((RT, LANE), lambda s: (rb(s), Z_LR // LANE)),
                pl.BlockSpec((2, LANE, qw), lambda s: (0, 0, 0)),
                pl.BlockSpec((2, 1, qw), lambda s: (0, 0, 0)),
                st_spec,
                pl.BlockSpec((1, GLA_DV), lambda s: (0, 0))]
    args = [z, z, z, z, wlr, blr, s0_t, nw.reshape(1, GLA_DV)]
    return pl.pallas_call(
        functools.partial(_gla_body, T=T, U=U),
        grid=(n_units // U,),
        in_specs=in_specs,
        out_specs=[pl.BlockSpec((RT, hv), lambda s: (s, 0)),
                   pl.BlockSpec((U, 2, GLA_H, GLA_DK, GLA_DV), lambda s: (s, 0, 0, 0, 0))],
        out_shape=[jax.ShapeDtypeStruct((n_units * T, hv), BF16),
                   jax.ShapeDtypeStruct((n_units, 2, GLA_H, GLA_DK, GLA_DV), F32)],
        scratch_shapes=[pltpu.VMEM((2, RT, qw), F32),
                        pltpu.VMEM((2, RT, hv), F32),
                        pltpu.VMEM((U, 2, GLA_H, GLA_DV, GLA_DK), F32),
                        pltpu.VMEM((2, RT, qw), BF16),
                        pltpu.VMEM((2, RT, qw), BF16),
                        pltpu.VMEM((2, RT // GLA_C, qw), F32)],
        compiler_params=_cparams(("arbitrary",)),
        name=f"gla_T{T}",
    )(*args)


def _fft_body(x_ref, w2_ref, dt_ref, y_ref, p_s, *, T, U):
    xb = x_ref[...]
    for g in range(FFT_G):
        cs = slice(g * FFT_CH, (g + 1) * FFT_CH)
        p = _dot(xb[:, cs], w2_ref[...])
        for u in range(U):
            rows = slice(u * T, (u + 1) * T)
            p_s[u, 0:T, cs] = p[rows, :FFT_CH].astype(BF16)
            p_s[u, T:2 * T, cs] = p[rows, FFT_CH:].astype(BF16)
    for u in range(U):
        y_ref[u * T:(u + 1) * T, :] = _dot(dt_ref[...], p_s[u]).astype(BF16)


def _dft_consts(T):
    c = np.arange(FFT_CH)
    ang_c = (np.outer(c, c) % FFT_CH) * (2.0 * np.pi / FFT_CH)
    w2 = np.concatenate([np.cos(ang_c), np.sin(ang_c)], axis=1) / np.sqrt(FFT_CH)
    t = np.arange(T)
    ang_t = (np.outer(t, t) % T) * (2.0 * np.pi / T)
    dt = np.concatenate([np.cos(ang_t), -np.sin(ang_t)], axis=1) / np.sqrt(T)
    return jnp.asarray(w2, F32).astype(BF16), jnp.asarray(dt, F32).astype(BF16)


FFT_UNITS = {CTX_T: 4, LAT_T: 1}


def _fft(z, *, T, n_units, unit0):
    w2, dt = _dft_consts(T)
    fw = FFT_G * FFT_CH
    U = FFT_UNITS[T]
    RT = U * T
    in_specs = [pl.BlockSpec((RT, fw), lambda s: (s + unit0 * T // RT, Z_FF // fw)),
                pl.BlockSpec((FFT_CH, 2 * FFT_CH), lambda s: (0, 0)),
                pl.BlockSpec((T, 2 * T), lambda s: (0, 0))]
    args = [z, w2, dt]
    return pl.pallas_call(
        functools.partial(_fft_body, T=T, U=U),
        grid=(n_units // U,),
        in_specs=in_specs,
        out_specs=pl.BlockSpec((RT, fw), lambda s: (s, 0)),
        out_shape=jax.ShapeDtypeStruct((n_units * T, fw), BF16),
        scratch_shapes=[pltpu.VMEM((U, 2 * T, fw), BF16)],
        compiler_params=_cparams(("arbitrary",)),
        name=f"fft_T{T}",
    )(*args)


def _rope_tables(T):
    half = HEAD_DIM // 2
    inv = ROPE_BASE ** (-np.arange(0, half, 2, dtype=np.float64) / half)
    t = np.arange(T)
    ang_r = (t // GRID_W)[:, None] * inv[None, :]
    ang_c = (t % GRID_W)[:, None] * inv[None, :]
    cos = np.concatenate([np.cos(ang_r)] * 2 + [np.cos(ang_c)] * 2, axis=1)
    sin = np.concatenate([-np.sin(ang_r), np.sin(ang_r), -np.sin(ang_c), np.sin(ang_c)], axis=1)
    return jnp.asarray(cos, F32), jnp.asarray(sin, F32)


def _rope(x, cos, sin):
    lane = lax.broadcasted_iota(jnp.int32, x.shape, 1)
    quarter = HEAD_DIM // 4
    partner = jnp.where(lane % (2 * quarter) < quarter,
                        pltpu.roll(x, HEAD_DIM - quarter, 1), pltpu.roll(x, quarter, 1))
    return x * cos + partner * sin


def _rms128(x, w):
    return x * lax.rsqrt(jnp.mean(x * x, axis=-1, keepdims=True) + EPS) * w


def _attn_body(*refs, T, U, latent):
    assert U == 1 or not latent
    if latent:
        (sink_ref, q_ref, k_ref, v_ref, qw_ref, kw_ref, ck_ref, cv_ref, cos_ref, sin_ref, y_ref) = refs
    else:
        (sink_ref, q_ref, k_ref, v_ref, qw_ref, kw_ref, y_ref, ko_ref, vo_ref) = refs
    kv = pl.program_id(1)
    nb = T // ATT_BLOCK
    kn = _rms128(k_ref[...].astype(F32), kw_ref[...])
    vb = v_ref[...]
    if latent:
        kn = _rope(kn, cos_ref[...], sin_ref[...])
        ckb = ck_ref[...].astype(BF16)
        cvb = cv_ref[...].astype(BF16)
    else:
        ko_ref[...] = kn
        vo_ref[...] = vb.astype(F32)
    kb = kn.astype(BF16)
    rows_g = lax.broadcasted_iota(jnp.int32, (GQA * ATT_BLOCK, 1), 0) // ATT_BLOCK
    sink = jnp.zeros((GQA * ATT_BLOCK, 1), F32)
    for g in range(GQA):
        sink = jnp.where(rows_g == g, sink_ref[kv * GQA + g], sink)
    scale = HEAD_DIM ** -0.5
    for u, n in [(u, n) for u in range(U) for n in range(nb)]:
        seq = slice(u * T, (u + 1) * T)
        rs = slice(u * T + n * ATT_BLOCK, u * T + (n + 1) * ATT_BLOCK)
        qs = []
        for g in range(GQA):
            qn = _rms128(q_ref[rs, g * HEAD_DIM:(g + 1) * HEAD_DIM].astype(F32), qw_ref[...])
            if latent:
                qn = _rope(qn, cos_ref[rs, :], sin_ref[rs, :])
            qs.append((qn * scale).astype(BF16))
        qq = jnp.concatenate(qs, axis=0)
        if latent:
            lo, hi = max(n - 1, 0), min(n + 2, nb)
            ks = slice(lo * ATT_BLOCK, hi * ATT_BLOCK)
            w = (hi - lo) * ATT_BLOCK
            s_loc = _dot_nt(qq, kb[ks])
            qpos = n * ATT_BLOCK + lax.broadcasted_iota(jnp.int32, (GQA * ATT_BLOCK, w), 0) % ATT_BLOCK
            kpos = lo * ATT_BLOCK + lax.broadcasted_iota(jnp.int32, (GQA * ATT_BLOCK, w), 1)
            s_loc = jnp.where(jnp.abs(kpos - qpos) <= WINDOW, s_loc, NEG_INF)
            s_ctx = _dot_nt(qq, ckb)
            m = jnp.maximum(jnp.maximum(jnp.max(s_loc, axis=-1, keepdims=True),
                                        jnp.max(s_ctx, axis=-1, keepdims=True)), sink)
            p_loc = jnp.exp(s_loc - m)
            p_ctx = jnp.exp(s_ctx - m)
            den = (jnp.sum(p_loc, axis=-1, keepdims=True) + jnp.sum(p_ctx, axis=-1, keepdims=True)
                   + jnp.exp(sink - m))
            o = _dot(p_loc.astype(BF16), vb[ks]) + _dot(p_ctx.astype(BF16), cvb)
        else:
            s = _dot_nt(qq, kb[seq])
            m = jnp.maximum(jnp.max(s, axis=-1, keepdims=True), sink)
            p = jnp.exp(s - m)
            den = jnp.sum(p, axis=-1, keepdims=True) + jnp.exp(sink - m)
            o = _dot(p.astype(BF16), vb[seq])
        o = o * (1.0 / den)
        for g in range(GQA):
            y_ref[rs, g * HEAD_DIM:(g + 1) * HEAD_DIM] = o[g * ATT_BLOCK:(g + 1) * ATT_BLOCK].astype(BF16)


ATT_UNITS = {CTX_T: 4, LAT_T: 1}


def _attn(z, sink_l, qw, kw, *, T, n_units, unit0, ctx_k=None, ctx_v=None):
    latent = ctx_k is not None
    qwid = GQA * HEAD_DIM
    U = ATT_UNITS[T]
    RT = U * T
    rb = lambda u: u + unit0 * T // RT
    smem = pl.BlockSpec(memory_space=pltpu.SMEM)
    in_specs = [smem,
                pl.BlockSpec((RT, qwid), lambda u, h: (rb(u), Z_AQ // qwid + h)),
                pl.BlockSpec((RT, HEAD_DIM), lambda u, h: (rb(u), Z_AK // HEAD_DIM + h)),
                pl.BlockSpec((RT, HEAD_DIM), lambda u, h: (rb(u), Z_AV // HEAD_DIM + h)),
                pl.BlockSpec((1, HEAD_DIM), lambda u, h: (0, 0)),
                pl.BlockSpec((1, HEAD_DIM), lambda u, h: (0, 0))]
    args = [sink_l, z, z, z, qw.reshape(1, HEAD_DIM), kw.reshape(1, HEAD_DIM)]
    y_spec = pl.BlockSpec((RT, qwid), lambda u, h: (u, h))
    y_shape = jax.ShapeDtypeStruct((n_units * T, N_HEADS * HEAD_DIM), BF16)
    if latent:
        cos, sin = _rope_tables(T)
        in_specs += [pl.BlockSpec((None, PAST, HEAD_DIM), lambda u, h: (u, 0, h)),
                     pl.BlockSpec((None, PAST, HEAD_DIM), lambda u, h: (u, 0, h)),
                     pl.BlockSpec((T, HEAD_DIM), lambda u, h: (0, 0)),
                     pl.BlockSpec((T, HEAD_DIM), lambda u, h: (0, 0))]
        args += [ctx_k, ctx_v, cos, sin]
        out_specs, out_shape = y_spec, y_shape
    else:
        kv_spec = pl.BlockSpec((RT, HEAD_DIM), lambda u, h: (u, h))
        kv_shape = jax.ShapeDtypeStruct((n_units * T, N_KV * HEAD_DIM), F32)
        out_specs, out_shape = [y_spec, kv_spec, kv_spec], [y_shape, kv_shape, kv_shape]
    return pl.pallas_call(
        functools.partial(_attn_body, T=T, U=U, latent=latent),
        grid=(n_units // U, N_KV),
        in_specs=in_specs,
        out_specs=out_specs,
        out_shape=out_shape,
        compiler_params=_cparams(("arbitrary", "arbitrary")),
        name=f"attn_T{T}",
    )(*args)


def _outproj_body(*refs, router, tm):
    y_refs, refs = refs[:6], refs[6:]
    if router:
        (w_ref, xa_ref, xb_ref, mod_ref, n2_ref, r_ref, x1_ref, h2_ref, idx_ref, gate_ref, cnt_ref, cnt_s) = refs
    else:
        (w_ref, xa_ref, xb_ref, mod_ref, n2_ref, x1_ref, h2_ref) = refs
    gw = GLA_H * GLA_DV
    fw = FFT_G * FFT_CH
    yg, yf, ya = (_pick_x(y_refs[2 * j], y_refs[2 * j + 1], tm) for j in range(3))
    acc = _dot(yg, w_ref[0:gw, :]) + _dot(yf, w_ref[gw:gw + fw, :]) + _dot(ya, w_ref[gw + fw:, :])
    x1 = _pick_x(xa_ref, xb_ref, tm) + mod_ref[2:3, :] * acc
    x1_ref[...] = x1
    h2 = _norm_mod(x1, n2_ref[...], mod_ref[4:5, :], mod_ref[3:4, :])
    h2_ref[...] = h2.astype(h2_ref.dtype)
    if router:
        logits = _dot(h2.astype(BF16), r_ref[...])
        lane = lax.broadcasted_iota(jnp.int32, logits.shape, 1)
        lg = jnp.where(lane < N_EXP, logits, -jnp.inf)
        v1 = jnp.max(lg, axis=-1, keepdims=True)
        i1 = jnp.min(jnp.where(lg == v1, lane, LANE), axis=-1, keepdims=True)
        lg2 = jnp.where(lane == i1, -jnp.inf, lg)
        v2 = jnp.max(lg2, axis=-1, keepdims=True)
        i2 = jnp.min(jnp.where(lg2 == v2, lane, LANE), axis=-1, keepdims=True)
        e = jnp.exp(v2 - v1)
        g1 = 1.0 / (1.0 + e)
        gate_ref[...] = jnp.where(lane == 0, g1, jnp.where(lane == 1, e * g1, 0.0))

        @pl.when(pl.program_id(0) == 0)
        def _():
            cnt_s[...] = jnp.zeros_like(cnt_s)

        oh1 = lane == i1
        oh2 = lane == i2
        oh = jnp.where(jnp.logical_or(oh1, oh2), 1.0, 0.0)
        ri = lax.broadcasted_iota(jnp.int32, (tm, tm), 0)
        ci = lax.broadcasted_iota(jnp.int32, (tm, tm), 1)
        before = jnp.where(ci < ri, 1.0, 0.0).astype(BF16)
        prior = _dot(before, oh.astype(BF16)) + cnt_s[0:1, :]
        r1 = jnp.sum(jnp.where(oh1, prior, 0.0), axis=-1, keepdims=True).astype(jnp.int32)
        r2 = jnp.sum(jnp.where(oh2, prior, 0.0), axis=-1, keepdims=True).astype(jnp.int32)
        idx_ref[...] = jnp.where(lane == 0, i1, jnp.where(lane == 1, i2,
                                 jnp.where(lane == 2, r1, jnp.where(lane == 3, r2, 0))))
        cnt_s[...] = cnt_s[...] + jnp.sum(oh, axis=0, keepdims=True)
        cnt_ref[...] = cnt_s[...]


def _outproj(y_gla, y_fft, y_att, w_out_b, x, mod, n2, router_b=None):
    tm = 512
    router = router_b is not None
    gw, fw, aw = GLA_H * GLA_DV, FFT_G * FFT_CH, N_HEADS * HEAD_DIM
    row = lambda i: (i, 0)
    xa, xb, off_b = _split_x(x, tm)
    in_specs, args = [], []
    for pair, wid in ((y_gla, gw), (y_fft, fw), (y_att, aw)):
        in_specs += _x_specs(tm, 0, wid)
        args += list(pair)
    in_specs.append(pl.BlockSpec((gw + fw + aw, D), lambda i: (0, 0), pipeline_mode=pl.Buffered(1)))
    in_specs += _x_specs(tm, off_b, D)
    in_specs += [pl.BlockSpec((None, 6, D), lambda i: (_mod_row(i, tm), 0, 0)),
                 pl.BlockSpec((1, D), lambda i: (0, 0))]
    args += [w_out_b, xa, xb, mod, n2.reshape(1, D)]
    out_specs = [pl.BlockSpec((tm, D), row), pl.BlockSpec((tm, D), row)]
    out_shape = [jax.ShapeDtypeStruct((N_TOK, D), F32),
                 jax.ShapeDtypeStruct((N_TOK, D), F32 if router else BF16)]
    if router:
        in_specs.append(pl.BlockSpec((D, LANE), lambda i: (0, 0)))
        args.append(router_b)
        out_specs += [pl.BlockSpec((tm, LANE), row), pl.BlockSpec((tm, LANE), row),
                      pl.BlockSpec((8, LANE), lambda i: (0, 0))]
        out_shape += [jax.ShapeDtypeStruct((N_TOK, LANE), jnp.int32), jax.ShapeDtypeStruct((N_TOK, LANE), F32),
                      jax.ShapeDtypeStruct((8, LANE), F32)]
    return pl.pallas_call(
        functools.partial(_outproj_body, router=router, tm=tm),
        grid=(N_TOK // tm,),
        in_specs=in_specs,
        out_specs=out_specs,
        out_shape=out_shape,
        scratch_shapes=[pltpu.VMEM((8, LANE), F32)] if router else [],
        compiler_params=_cparams(("arbitrary",)),
        name="outproj_router" if router else "outproj",
    )(*args)


FF_TF = 512
FF_KA = D_FF // FF_TF


def _store_f_slice(g_s, g, s, rows=slice(None)):
    for kk in range(FF_KA):
        @pl.when(s == kk)
        def _(kk=kk):
            g_s[rows, kk * FF_TF:(kk + 1) * FF_TF] = g


def _ffn_body(h_ref, w1_ref, w3_ref, w2_ref, x_ref, mod_ref, o_ref, g_s):
    s = pl.program_id(1)

    @pl.when(s < FF_KA)
    def _():
        h = h_ref[...]
        g = (_silu(_dot(h, w1_ref[...])) * _dot(h, w3_ref[...])).astype(BF16)
        _store_f_slice(g_s, g, s)

    @pl.when(s >= FF_KA)
    def _():
        o_ref[...] = x_ref[...] + mod_ref[5:6, :] * _dot(g_s[...], w2_ref[...])


def _ffn(h2, w1_b, w3_b, w2_b, x1, mod):
    tm, tn = 1024, 512
    kb = D // tn
    up = lambda i, s: (0, jnp.minimum(s, FF_KA - 1))
    down = lambda s: jnp.maximum(s - FF_KA, 0)
    return pl.pallas_call(
        _ffn_body,
        grid=(N_TOK // tm, FF_KA + kb),
        in_specs=[pl.BlockSpec((tm, D), lambda i, s: (i, 0)),
                  pl.BlockSpec((D, FF_TF), up),
                  pl.BlockSpec((D, FF_TF), up),
                  pl.BlockSpec((D_FF, tn), lambda i, s: (0, down(s))),
                  pl.BlockSpec((tm, tn), lambda i, s: (i, down(s))),
                  pl.BlockSpec((None, 6, tn), lambda i, s: (_mod_row(i, tm), 0, down(s)))],
        out_specs=pl.BlockSpec((tm, tn), lambda i, s: (i, down(s))),
        out_shape=jax.ShapeDtypeStruct((N_TOK, D), F32),
        scratch_shapes=[pltpu.VMEM((tm, D_FF), BF16)],
        compiler_params=_cparams(("arbitrary", "arbitrary")),
        name="ffn_dense",
    )(h2, w1_b, w3_b, w2_b, x1, mod)


MOE_TN = 256
MOE_KB = D // MOE_TN
MOE_GATHER = 256


def _moe_body(te_ref, tr_ref, src_ref, h_hbm, w1_ref, w3_ref, w2_ref, y_ref, xg_s, xb_s, g_s, sem):
    i = pl.program_id(0)
    s = pl.program_id(1)
    n_rows = tr_ref[i]
    parts = MOE_TM // MOE_GATHER
    cases = tuple((slice(0, q * MOE_GATHER),
                   jnp.logical_and(n_rows > (q - 1) * MOE_GATHER, n_rows <= q * MOE_GATHER))
                  for q in range(1, parts + 1))

    def row_copy(r, t):
        return pltpu.make_async_copy(h_hbm.at[pl.ds(t, 1)], xg_s.at[pl.ds(r, 1)], sem)

    for p in range(parts):
        @pl.when(jnp.logical_and(n_rows > p * MOE_GATHER, s == 0))
        def _(p=p):
            base = i * MOE_TM + p * MOE_GATHER

            def issue(r2, c):
                for q in range(2):
                    r = 2 * r2 + q
                    row_copy(r, src_ref[base + r]).start(priority=q)
                return c

            lax.fori_loop(0, MOE_GATHER // 2, issue, 0, unroll=4)
            pltpu.make_async_copy(h_hbm.at[pl.ds(0, MOE_GATHER)], xg_s, sem).wait()
            xb_s[p * MOE_GATHER:(p + 1) * MOE_GATHER, :] = xg_s[...].astype(BF16)

    for rows, cond in cases:
        @pl.when(jnp.logical_and(cond, s < FF_KA))
        def _(rows=rows):
            h = xb_s[rows, :]
            g = (_silu(_dot(h, w1_ref[...].astype(BF16))) * _dot(h, w3_ref[...].astype(BF16))).astype(BF16)
            _store_f_slice(g_s, g, s, rows)

        @pl.when(jnp.logical_and(cond, s >= FF_KA))
        def _(rows=rows):
            y_ref[rows, :] = _dot(g_s[rows, :], w2_ref[...].astype(BF16))

    for p in range(parts):
        @pl.when(n_rows <= p * MOE_GATHER)
        def _(p=p):
            y_ref[p * MOE_GATHER:(p + 1) * MOE_GATHER, :] = jnp.zeros((MOE_GATHER, MOE_TN), F32)


def _moe(tile_exp, tile_rows, row_src, h2, w1, w3, w2):
    def up(i, s, te, tr, src):
        return (te[i], 0, jnp.where(tr[i] > 0, jnp.minimum(s, FF_KA - 1), FF_KA - 1))

    def down(i, s, te, tr, src):
        return (te[i], 0, jnp.where(tr[i] > 0, jnp.maximum(s - FF_KA, 0), MOE_KB - 1))

    return pl.pallas_call(
        _moe_body,
        grid_spec=pltpu.PrefetchScalarGridSpec(
            num_scalar_prefetch=3,
            grid=(MOE_TILES, FF_KA + MOE_KB),
            in_specs=[pl.BlockSpec(memory_space=pl.ANY),
                      pl.BlockSpec((None, D, FF_TF), up),
                      pl.BlockSpec((None, D, FF_TF), up),
                      pl.BlockSpec((None, D_FF, MOE_TN), down)],
            out_specs=pl.BlockSpec((MOE_TM, MOE_TN), lambda i, s, te, tr, src: (i, jnp.maximum(s - FF_KA, 0))),
            scratch_shapes=[pltpu.VMEM((MOE_GATHER, D), F32), pltpu.VMEM((MOE_TM, D), BF16),
                            pltpu.VMEM((MOE_TM, D_FF), BF16), pltpu.SemaphoreType.DMA(())],
        ),
        out_shape=jax.ShapeDtypeStruct((MOE_ROWS, D), F32),
        compiler_params=_cparams(("arbitrary", "arbitrary")),
        name="moe_experts",
    )(tile_exp, tile_rows, row_src, h2, w1, w3, w2)


def _combine_body(slot_ref, y_hbm, x_ref, g_ref, mod_ref, oc_ref, ol_ref, buf, sems, *, tc):
    i = pl.program_id(0)

    def gather(tile, b):
        base = tile * tc

        def issue(r, c):
            for k in range(2):
                pltpu.make_async_copy(y_hbm.at[pl.ds(slot_ref[2 * (base + r) + k], 1)],
                                      buf.at[b, k, pl.ds(r, 1)], sems.at[b]).start(priority=k)
            return c

        lax.fori_loop(0, tc, issue, 0, unroll=8)

    def wait(b):
        for k in range(2):
            pltpu.make_async_copy(y_hbm.at[pl.ds(0, tc)], buf.at[b, k], sems.at[b]).wait()

    @pl.when(i == 0)
    def _():
        gather(0, 0)

    for b in range(2):
        @pl.when(jnp.logical_and(i % 2 == b, i + 1 < pl.num_programs(0)))
        def _(b=b):
            gather(i + 1, 1 - b)

    for b in range(2):
        @pl.when(i % 2 == b)
        def _(b=b):
            wait(b)
            g = g_ref[...]
            f = g[:, 0:1] * buf[b, 0] + g[:, 1:2] * buf[b, 1]
            _combine_store(x_ref[...] + mod_ref[5:6, :] * f, i, oc_ref, ol_ref, tc)


def _combine_store(out, i, oc_ref, ol_ref, tc):
    is_ctx = i < N_CTX // tc

    @pl.when(is_ctx)
    def _():
        oc_ref[...] = out

    @pl.when(jnp.logical_not(is_ctx))
    def _():
        ol_ref[...] = out


def _combine(slot, y, x1, gates, mod):
    tc = 256
    n_ctx_tiles = N_CTX // tc
    return pl.pallas_call(
        functools.partial(_combine_body, tc=tc),
        grid_spec=pltpu.PrefetchScalarGridSpec(
            num_scalar_prefetch=1,
            grid=(N_TOK // tc,),
            in_specs=[pl.BlockSpec(memory_space=pl.ANY),
                      pl.BlockSpec((tc, D), lambda i, s: (i, 0)),
                      pl.BlockSpec((tc, LANE), lambda i, s: (i, 0)),
                      pl.BlockSpec((None, 6, D), lambda i, s: (_mod_row(i, tc), 0, 0))],
            out_specs=[pl.BlockSpec((tc, D), lambda i, s: (jnp.minimum(i, n_ctx_tiles - 1), 0)),
                       pl.BlockSpec((tc, D), lambda i, s: (jnp.maximum(i - n_ctx_tiles, 0), 0))],
            scratch_shapes=[pltpu.VMEM((2, 2, tc, D), F32), pltpu.SemaphoreType.DMA((2,))],
        ),
        out_shape=[jax.ShapeDtypeStruct((N_CTX, D), F32), jax.ShapeDtypeStruct((N_LAT, D), F32)],
        compiler_params=_cparams(("arbitrary",)),
        name="moe_combine",
    )(slot, y, x1, gates, mod)


def _route_meta(idx4, counts):
    counts = counts.astype(jnp.int32)
    tiles = (counts + MOE_TM - 1) // MOE_TM
    tile_end = jnp.cumsum(tiles)
    tile_start = tile_end - tiles
    n_tiles = tile_end[-1]
    exp_of = idx4[:, 0:2].reshape(-1)
    rank = idx4[:, 2:4].reshape(-1)
    first_row = jnp.sum(jnp.where(exp_of[:, None] == jnp.arange(N_EXP, dtype=jnp.int32)[None, :],
                                  (tile_start * MOE_TM)[None, :], 0), axis=1)
    slot = (first_row + rank).astype(jnp.int32)
    j = jnp.arange(MOE_TILES, dtype=jnp.int32)
    te = jnp.sum((j[:, None] >= tile_end[None, :]).astype(jnp.int32), axis=1)
    te = jnp.where(j < n_tiles, te, te[jnp.maximum(n_tiles - 1, 0)])
    te = jnp.minimum(te, N_EXP - 1).astype(jnp.int32)
    rows = jnp.clip(counts[te] - (j - tile_start[te]) * MOE_TM, 0, MOE_TM)
    rows = jnp.where(j < n_tiles, rows, 0).astype(jnp.int32)
    tok = jnp.arange(2 * N_TOK, dtype=jnp.int32) // 2
    row_src = jnp.zeros((MOE_ROWS,), jnp.int32).at[slot].set(tok, unique_indices=True)
    return te, rows, row_src, slot


IN_W = 3616
LR_AT = Z_FF


def _prep_w_in_body(w_ref, o_ref):
    lr_w = 2 * GLA_RANK
    rows = w_ref.shape[0]
    o_ref[:, 0:LR_AT] = w_ref[:, 0:LR_AT].astype(BF16)
    o_ref[:, LR_AT:Z_LR] = w_ref[:, LR_AT + lr_w:IN_W].astype(BF16)
    tail = jnp.concatenate([w_ref[:, LR_AT:LR_AT + lr_w], jnp.zeros((rows, LANE - lr_w), F32)], axis=1)
    o_ref[:, Z_LR:Z_LR + LANE] = tail.astype(BF16)
    o_ref[:, Z_LR + LANE:] = jnp.zeros((rows, Z_W - Z_LR - LANE), BF16)


def _prep_w_in(w_all, l):
    tr = 256
    return pl.pallas_call(
        _prep_w_in_body,
        grid=(D // tr,),
        in_specs=[pl.BlockSpec((None, tr, IN_W), lambda i: (l, i, 0))],
        out_specs=pl.BlockSpec((tr, Z_W), lambda i: (i, 0)),
        out_shape=jax.ShapeDtypeStruct((D, Z_W), BF16),
        compiler_params=_cparams(("arbitrary",)),
        name="prep_w_in",
    )(w_all)


def _prep_w_lr(w_lr2):
    out = jnp.zeros((2, LANE, GLA_H * GLA_DK), F32)
    out = out.at[0, 0:GLA_RANK].set(w_lr2[0]).at[1, GLA_RANK:2 * GLA_RANK].set(w_lr2[1])
    return out.astype(BF16)


def kernel(x_prompt, x_sample, cache_k, cache_v, state_gla, c, c_ctx, w_ada, b_ada, norm1_w, norm2_w, w_in, w_gla_lr2, b_gla_lr2, gla_norm_w, q_norm_w, k_norm_w, attn_sink, w_out, ffn_w1, ffn_w3, ffn_w2, moe_router, moe_w1, moe_w3, moe_w2):
    x = (x_prompt.reshape(N_CTX, D), x_sample.reshape(N_LAT, D))
    cond16 = jnp.concatenate([c_ctx[None, :], c, jnp.zeros((16 - 1 - N_LAT_B, D), F32)], axis=0)
    mods = _mods(cond16, w_ada, b_ada).reshape(DEPTH, 16, 6, D)
    ck_all = cache_k.reshape(N_LAT_B, DEPTH, PAST, N_KV * HEAD_DIM)
    cv_all = cache_v.reshape(N_LAT_B, DEPTH, PAST, N_KV * HEAD_DIM)
    s0_all = jnp.swapaxes(state_gla, -1, -2)
    zero_state = jnp.zeros((N_CTX_B, 2, GLA_H, GLA_DV, GLA_DK), F32)
    ctx_units = dict(T=CTX_T, n_units=N_CTX_B, unit0=0)
    lat_units = dict(T=LAT_T, n_units=N_LAT_B, unit0=N_CTX // LAT_T)

    ks, vs, ss = [], [], []
    for l in range(DEPTH):
        mod = mods[l]
        z = _inproj(x, mod, norm1_w[l], _prep_w_in(w_in, l))
        wlr = _prep_w_lr(w_gla_lr2[l])
        blr = b_gla_lr2[l].reshape(2, 1, GLA_H * GLA_DK)
        yg_c, st_c = _gla(z, wlr, blr, zero_state, gla_norm_w[l], **ctx_units)
        yg_l, _ = _gla(z, wlr, blr, s0_all[:, l], gla_norm_w[l], **lat_units)
        y_gla = (yg_c, yg_l)
        y_fft = (_fft(z, **ctx_units), _fft(z, **lat_units))
        ya_c, k_c, v_c = _attn(z, attn_sink[l], q_norm_w[l], k_norm_w[l], **ctx_units)
        ya_l = _attn(z, attn_sink[l], q_norm_w[l], k_norm_w[l], ctx_k=ck_all[:, l], ctx_v=cv_all[:, l],
                     **lat_units)
        y_att = (ya_c, ya_l)
        w_out_b = w_out[l].astype(BF16)
        e = l // 2
        if l % 2 == 0:
            x1, h2 = _outproj(y_gla, y_fft, y_att, w_out_b, x, mod, norm2_w[l])
            x = _ffn(h2, ffn_w1[e].astype(BF16), ffn_w3[e].astype(BF16), ffn_w2[e].astype(BF16), x1, mod)
        else:
            router_b = jnp.pad(moe_router[e], ((0, 0), (0, LANE - N_EXP))).astype(BF16)
            x1, h2, idx, gates, counts = _outproj(y_gla, y_fft, y_att, w_out_b, x, mod, norm2_w[l], router_b)
            te, tile_rows, row_src, slot = _route_meta(idx[:, :4], counts[0, :N_EXP])
            y = _moe(te, tile_rows, row_src, h2, moe_w1[e], moe_w3[e], moe_w2[e])
            x = _combine(slot, y, x1, gates, mod)
        ks.append(k_c.reshape(N_CTX_B, CTX_T, N_KV, HEAD_DIM))
        vs.append(v_c.reshape(N_CTX_B, CTX_T, N_KV, HEAD_DIM))
        ss.append(st_c)
    if not isinstance(x, (tuple, list)):
        x = (x[:N_CTX], x[N_CTX:])
    y_prompt = x[0].reshape(N_CTX_B, CTX_T, D)
    y_sample = x[1].reshape(N_LAT_B, LAT_T, D)
    return (y_prompt, y_sample, jnp.stack(ks, axis=1), jnp.stack(vs, axis=1), jnp.stack(ss, axis=1))
```
